```python
import jax, jax.numpy as jnp
from jax import lax
import numpy as np

D_MODEL = 2048
BATCH = 4
SEQ = 2048
DEPTH = 4
DEC_BATCH = 8
DEC_SEQ = 4
PAST_LEN = 16384
PAGE_SIZE = 128

N_GLA = (DEPTH + 1) // 2
N_FOX = DEPTH // 2
GLA_HEADS = 4
GLA_DK = D_MODEL // 2
GLA_DV = D_MODEL
GLA_DKH = GLA_DK // GLA_HEADS
GLA_DVH = GLA_DV // GLA_HEADS
GLA_GATE_RANK = 16
GLA_GATE_NORM = 16.0
GLA_CHUNK = 64
GLA_IN = 2 * GLA_DK + 2 * GLA_DV + GLA_GATE_RANK
FOX_HEADS = 16
FOX_HD = D_MODEL // FOX_HEADS
FOX_DIM = FOX_HEADS * FOX_HD
FOX_QBLOCK = 128
FOX_IN = 4 * FOX_DIM + FOX_HEADS
FOX_FBIAS_LO = 2.0
FOX_FBIAS_HI = 10.0
MOE_GROUPS = 4
MOE_PER_GROUP = 8
MOE_EXPERTS = MOE_GROUPS * MOE_PER_GROUP
MOE_TOPK = 2
MOE_DFF = D_MODEL // 4
MOE_BLOCK = 128

RMS_EPS = 1e-6
NEG_INF = -1e30

kernel_name = 'hybrid_gla_fox_adaln_hmoe_step'


def rms_norm(x, g):
    xf = x.astype(jnp.float32)
    y = xf * lax.rsqrt(jnp.mean(xf * xf, axis=-1, keepdims=True) + RMS_EPS)
    return (y * g.astype(jnp.float32)).astype(x.dtype)


def ada_mod(c, w, b):
    m = jax.nn.silu(c) @ w + b
    return jnp.split(m[:, None, :], 6, axis=-1)


def modulate(x, g, shift, scale):
    return rms_norm(x, g) * (1 + scale) + shift


def gla_project(h, w_in, w_gate_up, b_gate):
    B, T, _ = h.shape
    z = h @ w_in
    q, k, v, g, gd = jnp.split(z, [GLA_DK, 2 * GLA_DK, 2 * GLA_DK + GLA_DV, 2 * GLA_DK + 2 * GLA_DV], axis=-1)
    q = q.reshape(B, T, GLA_HEADS, GLA_DKH) * (GLA_DKH ** -0.5)
    k = k.reshape(B, T, GLA_HEADS, GLA_DKH)
    v = v.reshape(B, T, GLA_HEADS, GLA_DVH)
    lg = jax.nn.log_sigmoid((gd @ w_gate_up + b_gate).astype(jnp.float32)) / GLA_GATE_NORM
    lg = lg.reshape(B, T, GLA_HEADS, GLA_DKH)
    return q, k, v, g, lg


def gla_chunk(S, q, k, v, lg):
    qf, kf, vf = q.astype(jnp.float32), k.astype(jnp.float32), v.astype(jnp.float32)
    C = q.shape[1]
    b = jnp.cumsum(lg, axis=1)
    causal = jnp.tril(jnp.ones((C, C), dtype=bool))
    inter = jnp.einsum('bthk,bhkv->bthv', qf * jnp.exp(b), S)
    rel = b[:, :, None] - b[:, None]
    decay = jnp.exp(jnp.where(causal[None, :, :, None, None], rel, NEG_INF))
    a = jnp.einsum('bthk,bshk,btshk->bhts', qf, kf, decay)
    intra = jnp.einsum('bhts,bshv->bthv', a, vf)
    b_last = b[:, -1]
    S_new = jnp.exp(b_last)[..., None] * S + jnp.einsum('bshk,bshv->bhkv', kf * jnp.exp(b_last[:, None] - b), vf)
    return inter + intra, S_new


def gla_prompt_scan(q, k, v, lg):
    B, T = q.shape[:2]
    nc = T // GLA_CHUNK

    def to_chunks(a):
        return jnp.moveaxis(a.reshape((B, nc, GLA_CHUNK) + a.shape[2:]), 1, 0)

    def step(S, xs):
        o, S = gla_chunk(S, *xs)
        return S, o

    S0 = jnp.zeros((B, GLA_HEADS, GLA_DKH, GLA_DVH), jnp.float32)
    S_fin, o = lax.scan(step, S0, (to_chunks(q), to_chunks(k), to_chunks(v), to_chunks(lg)))
    o = jnp.moveaxis(o, 0, 1).reshape(B, T, GLA_HEADS, GLA_DVH)
    return o, S_fin


def gla_output(o, g, norm_g, w_out):
    B, T = o.shape[:2]
    o = rms_norm(o, norm_g).reshape(B, T, GLA_DV)
    o = o * jax.nn.silu(g.astype(jnp.float32))
    return o.astype(g.dtype) @ w_out


def fox_project(h, w_in, b_f, q_norm, k_norm):
    B, T, _ = h.shape
    z = h @ w_in
    q, k, v, g, f = jnp.split(z, [FOX_DIM, 2 * FOX_DIM, 3 * FOX_DIM, 4 * FOX_DIM], axis=-1)
    q = rms_norm(q.reshape(B, T, FOX_HEADS, FOX_HD), q_norm)
    k = rms_norm(k.reshape(B, T, FOX_HEADS, FOX_HD), k_norm)
    v = v.reshape(B, T, FOX_HEADS, FOX_HD)
    logf = jax.nn.log_sigmoid(f.astype(jnp.float32) + b_f.astype(jnp.float32))
    return q, k, v, g, logf


def fox_prompt_attend(q, k, v, logf):
    B, T = q.shape[:2]
    nb = T // FOX_QBLOCK
    scale = FOX_HD ** -0.5
    cum = jnp.cumsum(logf, axis=1)
    cum_k = cum.transpose(0, 2, 1)
    qb = jnp.moveaxis(q.reshape(B, nb, FOX_QBLOCK, FOX_HEADS, FOX_HD), 1, 0)
    cb = jnp.moveaxis(cum.reshape(B, nb, FOX_QBLOCK, FOX_HEADS), 1, 0)
    pos = jnp.arange(T).reshape(nb, FOX_QBLOCK)
    kpos = jnp.arange(T)

    def block(args):
        qi, ci, pi = args
        s = jnp.einsum('bqhd,bkhd->bhqk', qi, k).astype(jnp.float32) * scale
        bias = ci.transpose(0, 2, 1)[:, :, :, None] - cum_k[:, :, None, :]
        s = jnp.where(kpos[None, :] <= pi[:, None], s + bias, NEG_INF)
        p = jax.nn.softmax(s, axis=-1)
        return jnp.einsum('bhqk,bkhd->bqhd', p.astype(v.dtype), v)

    o = lax.map(block, (qb, cb, pos))
    return jnp.moveaxis(o, 0, 1).reshape(B, T, FOX_HEADS, FOX_HD)


def fox_sample_attend(q, k, v, logf, ck, cv, clf, page_table):
    DB, S = q.shape[:2]
    past = page_table.shape[1] * PAGE_SIZE
    scale = FOX_HD ** -0.5
    k_past = ck[page_table].reshape(DB, past, FOX_HEADS, FOX_HD)
    v_past = cv[page_table].reshape(DB, past, FOX_HEADS, FOX_HD)
    lf_past = clf[page_table].reshape(DB, past, FOX_HEADS).astype(jnp.float32)
    suf = lax.cumsum(lf_past, axis=1, reverse=True) - lf_past
    cum = jnp.cumsum(logf, axis=1)
    cum_t = cum.transpose(0, 2, 1)
    s_past = jnp.einsum('bqhd,bkhd->bhqk', q, k_past).astype(jnp.float32) * scale
    s_past = s_past + cum_t[:, :, :, None] + suf.transpose(0, 2, 1)[:, :, None, :]
    s_new = jnp.einsum('bqhd,bkhd->bhqk', q, k).astype(jnp.float32) * scale
    s_new = s_new + cum_t[:, :, :, None] - cum_t[:, :, None, :]
    causal = jnp.tril(jnp.ones((S, S), dtype=bool))
    s_new = jnp.where(causal, s_new, NEG_INF)
    p = jax.nn.softmax(jnp.concatenate([s_past, s_new], axis=-1), axis=-1).astype(v.dtype)
    o = jnp.einsum('bhqk,bkhd->bqhd', p[..., :past], v_past) + jnp.einsum('bhqk,bkhd->bqhd', p[..., past:], v)
    return o


def fox_output(o, g, w_out):
    B, T = o.shape[:2]
    return (o.reshape(B, T, FOX_DIM) * jax.nn.sigmoid(g)) @ w_out


def expert_mlp(x, wg, wu, wd):
    return (jax.nn.silu(x @ wg) * (x @ wu)) @ wd


def moe(h3, w_group, b_group, w_expert, b_expert, w_gate, w_up, w_down):
    B, T, D = h3.shape
    h = h3.reshape(B * T, D)
    n_tok = B * T
    pg = jax.nn.softmax((h @ w_group).astype(jnp.float32) + b_group.astype(jnp.float32), axis=-1)
    g_idx = jnp.argmax(pg, axis=-1)
    p_g = jnp.max(pg, axis=-1)
    le = ((h @ w_expert).astype(jnp.float32) + b_expert.astype(jnp.float32)).reshape(n_tok, MOE_GROUPS, MOE_PER_GROUP)
    le_g = jnp.take_along_axis(le, g_idx[:, None, None], axis=1)[:, 0]
    pe = jax.nn.softmax(le_g, axis=-1)
    top_w, top_i = lax.top_k(pe, MOE_TOPK)
    top_w = top_w / jnp.sum(top_w, axis=-1, keepdims=True)
    gate_w = (p_g[:, None] * top_w).reshape(-1)
    e_ids = (g_idx[:, None] * MOE_PER_GROUP + top_i).reshape(-1)
    tok = jnp.repeat(jnp.arange(n_tok), MOE_TOPK)
    n_assign = n_tok * MOE_TOPK
    order = jnp.argsort(e_ids)
    se, stok, sw = e_ids[order], tok[order], gate_w[order]
    counts = jnp.zeros((MOE_EXPERTS,), jnp.int32).at[e_ids].add(1)
    start = jnp.cumsum(counts) - counts
    padded = ((counts + MOE_BLOCK - 1) // MOE_BLOCK) * MOE_BLOCK
    pend = jnp.cumsum(padded)
    pstart = pend - padded
    dest = pstart[se] + (jnp.arange(n_assign) - start[se])
    n_blocks = (n_assign + MOE_EXPERTS * (MOE_BLOCK - 1) + MOE_BLOCK - 1) // MOE_BLOCK
    cap = n_blocks * MOE_BLOCK
    buf_tok = jnp.zeros((cap,), jnp.int32).at[dest].set(stok.astype(jnp.int32))
    buf_w = jnp.zeros((cap,), jnp.float32).at[dest].set(sw)
    block_e = jnp.minimum(jnp.searchsorted(pend, jnp.arange(n_blocks) * MOE_BLOCK, side='right'), MOE_EXPERTS - 1)
    xb = h[buf_tok].reshape(n_blocks, MOE_BLOCK, D)

    def run_block(args):
        xblk, e = args
        return expert_mlp(xblk, w_gate[e], w_up[e], w_down[e])

    yb = lax.map(run_block, (xb, block_e)).reshape(cap, D)
    out = jnp.zeros((n_tok, D), jnp.float32).at[buf_tok].add(yb.astype(jnp.float32) * buf_w[:, None])
    return out.astype(h3.dtype).reshape(B, T, D)


def setup_inputs(seed: int = 0) -> dict:
    key = jax.random.key(seed)
    ks = jax.random.split(key, 31)

    def nrm(k, shape, scale=1.0):
        return jax.random.normal(k, shape, jnp.float32) * scale

    n_pages = PAST_LEN // PAGE_SIZE
    n_used = DEC_BATCH * n_pages
    n_pool = n_used + n_used // 4
    page_table = jax.random.permutation(ks[8], n_pool)[:n_used].reshape(DEC_BATCH, n_pages).astype(jnp.int32)
    d = D_MODEL
    head_bias = jnp.linspace(FOX_FBIAS_LO, FOX_FBIAS_HI, FOX_HEADS, dtype=jnp.float32)
    return {
        'x_prompt': nrm(ks[0], (BATCH, SEQ, d)),
        'x_sample': nrm(ks[1], (DEC_BATCH, DEC_SEQ, d)),
        'c_prompt': nrm(ks[2], (BATCH, d)),
        'c_sample': nrm(ks[3], (DEC_BATCH, d)),
        'state_gla': nrm(ks[4], (N_GLA, DEC_BATCH, GLA_HEADS, GLA_DKH, GLA_DVH)),
        'cache_k': nrm(ks[5], (N_FOX, n_pool, PAGE_SIZE, FOX_HEADS, FOX_HD)),
        'cache_v': nrm(ks[6], (N_FOX, n_pool, PAGE_SIZE, FOX_HEADS, FOX_HD)),
        'cache_logf': jax.nn.log_sigmoid(head_bias + nrm(ks[7], (N_FOX, n_pool, PAGE_SIZE, FOX_HEADS), 0.5)),
        'page_table': page_table,
        'norm_mix': 1.0 + nrm(ks[9], (DEPTH, d), 0.02),
        'norm_ffn': 1.0 + nrm(ks[10], (DEPTH, d), 0.02),
        'norm_final': 1.0 + nrm(ks[11], (d,), 0.02),
        'w_ada': nrm(ks[12], (DEPTH, d, 6 * d), 0.5 * d ** -0.5),
        'b_ada': nrm(ks[13], (DEPTH, 6 * d), 0.02),
        'gla_w_in': nrm(ks[14], (N_GLA, d, GLA_IN), d ** -0.5),
        'gla_w_gate_up': nrm(ks[15], (N_GLA, GLA_GATE_RANK, GLA_DK), GLA_GATE_RANK ** -0.5),
        'gla_b_gate': nrm(ks[16], (N_GLA, GLA_DK), 0.1),
        'gla_norm': 1.0 + nrm(ks[17], (N_GLA, GLA_DVH), 0.02),
        'gla_w_out': nrm(ks[18], (N_GLA, GLA_DV, d), GLA_DV ** -0.5),
        'fox_w_in': nrm(ks[19], (N_FOX, d, FOX_IN), d ** -0.5),
        'fox_b_f': head_bias + nrm(ks[20], (N_FOX, FOX_HEADS), 0.3),
        'fox_q_norm': 1.0 + nrm(ks[21], (N_FOX, FOX_HD), 0.02),
        'fox_k_norm': 1.0 + nrm(ks[22], (N_FOX, FOX_HD), 0.02),
        'fox_w_out': nrm(ks[23], (N_FOX, FOX_DIM, d), FOX_DIM ** -0.5),
        'moe_w_group': nrm(ks[24], (DEPTH, d, MOE_GROUPS), d ** -0.5),
        'moe_b_group': nrm(ks[25], (DEPTH, MOE_GROUPS), 0.01),
        'moe_w_expert': nrm(ks[26], (DEPTH, d, MOE_EXPERTS), d ** -0.5),
        'moe_b_expert': nrm(ks[27], (DEPTH, MOE_EXPERTS), 0.01),
        'moe_w_gate': nrm(ks[28], (DEPTH, MOE_EXPERTS, d, MOE_DFF), d ** -0.5),
        'moe_w_up': nrm(ks[29], (DEPTH, MOE_EXPERTS, d, MOE_DFF), d ** -0.5),
        'moe_w_down': nrm(ks[30], (DEPTH, MOE_EXPERTS, MOE_DFF, d), MOE_DFF ** -0.5),
    }


def reference(x_prompt, x_sample, c_prompt, c_sample, state_gla, cache_k, cache_v, cache_logf, page_table,
              norm_mix, norm_ffn, norm_final, w_ada, b_ada,
              gla_w_in, gla_w_gate_up, gla_b_gate, gla_norm, gla_w_out,
              fox_w_in, fox_b_f, fox_q_norm, fox_k_norm, fox_w_out,
              moe_w_group, moe_b_group, moe_w_expert, moe_b_expert, moe_w_gate, moe_w_up, moe_w_down):
    xp, xs = x_prompt, x_sample
    gla_p, gla_s = [], []
    kp_l, vp_l, lfp_l, ks_l, vs_l, lfs_l = [], [], [], [], [], []
    for i in range(DEPTH):
        j = i // 2
        sh1p, sc1p, ga1p, sh2p, sc2p, ga2p = ada_mod(c_prompt, w_ada[i], b_ada[i])
        sh1s, sc1s, ga1s, sh2s, sc2s, ga2s = ada_mod(c_sample, w_ada[i], b_ada[i])
        hp = modulate(xp, norm_mix[i], sh1p, sc1p)
        hs = modulate(xs, norm_mix[i], sh1s, sc1s)
        if i % 2 == 0:
            q, k, v, g, lg = gla_project(hp, gla_w_in[j], gla_w_gate_up[j], gla_b_gate[j])
            o, s_fin = gla_prompt_scan(q, k, v, lg)
            mp = gla_output(o, g, gla_norm[j], gla_w_out[j])
            gla_p.append(s_fin)
            q, k, v, g, lg = gla_project(hs, gla_w_in[j], gla_w_gate_up[j], gla_b_gate[j])
            o, s_new = gla_chunk(state_gla[j].astype(jnp.float32), q, k, v, lg)
            ms = gla_output(o, g, gla_norm[j], gla_w_out[j])
            gla_s.append(s_new)
        else:
            q, k, v, g, lf = fox_project(hp, fox_w_in[j], fox_b_f[j], fox_q_norm[j], fox_k_norm[j])
            mp = fox_output(fox_prompt_attend(q, k, v, lf), g, fox_w_out[j])
            kp_l.append(k)
            vp_l.append(v)
            lfp_l.append(lf)
            q, k, v, g, lf = fox_project(hs, fox_w_in[j], fox_b_f[j], fox_q_norm[j], fox_k_norm[j])
            o = fox_sample_attend(q, k, v, lf, cache_k[j], cache_v[j], cache_logf[j], page_table)
            ms = fox_output(o, g, fox_w_out[j])
            ks_l.append(k)
            vs_l.append(v)
            lfs_l.append(lf)
        xp = xp + ga1p * mp
        xs = xs + ga1s * ms
        hp = modulate(xp, norm_ffn[i], sh2p, sc2p)
        hs = modulate(xs, norm_ffn[i], sh2s, sc2s)
        xp = xp + ga2p * moe(hp, moe_w_group[i], moe_b_group[i], moe_w_expert[i], moe_b_expert[i],
                             moe_w_gate[i], moe_w_up[i], moe_w_down[i])
        xs = xs + ga2s * moe(hs, moe_w_group[i], moe_b_group[i], moe_w_expert[i], moe_b_expert[i],
                             moe_w_gate[i], moe_w_up[i], moe_w_down[i])
    y_prompt = rms_norm(xp, norm_final)
    y_sample = rms_norm(xs, norm_final)
    return (y_prompt, y_sample,
            jnp.stack(kp_l), jnp.stack(vp_l), jnp.stack(lfp_l), jnp.stack(gla_p),
            jnp.stack(ks_l), jnp.stack(vs_l), jnp.stack(lfs_l), jnp.stack(gla_s))
```

```python
import functools

import jax
import jax.numpy as jnp
from jax import lax
from jax.experimental import pallas as pl
from jax.experimental.pallas import tpu as pltpu

F32 = jnp.float32
BF16 = jnp.bfloat16
HIGHEST = lax.Precision.HIGHEST

D_MODEL = 2048
GLA_HEADS = 4
GLA_DK = D_MODEL // 2
GLA_DV = D_MODEL
GLA_DKH = GLA_DK // GLA_HEADS
GLA_DVH = GLA_DV // GLA_HEADS
GLA_GATE_RANK = 16
GLA_GATE_NORM = 16.0
GLA_MAIN = 2 * GLA_DK + 2 * GLA_DV
FOX_HEADS = 16
FOX_HD = D_MODEL // FOX_HEADS
FOX_DIM = FOX_HEADS * FOX_HD
MOE_GROUPS = 4
MOE_PER_GROUP = 8
MOE_EXPERTS = MOE_GROUPS * MOE_PER_GROUP
MOE_TOPK = 2
MOE_DFF = D_MODEL // 4
RMS_EPS = 1e-6
NEG_INF = -1e30

VMEM_LIMIT_BYTES = 52 * 1024 * 1024
LANES = 128

GLA_CHUNK = 128
GLA_SUB = 16
ATTN_BLOCK = 256
PAGES_PER_STEP = 4
MOE_BLOCK = 128
ROUTER_COLS = 128


def _params(*sem):
    return pltpu.CompilerParams(dimension_semantics=sem, vmem_limit_bytes=VMEM_LIMIT_BYTES)


def _log_sigmoid(x):
    return jnp.minimum(x, 0.0) - jnp.log1p(jnp.exp(-jnp.abs(x)))


def _silu(x):
    return x * jax.nn.sigmoid(x)


def _split3(a):
    hi = a.astype(BF16)
    r1 = a - hi.astype(F32)
    mid = r1.astype(BF16)
    lo = (r1 - mid.astype(F32)).astype(BF16)
    return hi, mid, lo


def _dot_exact_lhs(sel_bf16, x_f32):
    hi, mid, lo = _split3(x_f32)
    d = functools.partial(jnp.dot, preferred_element_type=F32)
    return d(sel_bf16, hi) + d(sel_bf16, mid) + d(sel_bf16, lo)


def _dot_exact_rhs(x_f32, sel_bf16):
    hi, mid, lo = _split3(x_f32)
    d = functools.partial(jnp.dot, preferred_element_type=F32)
    return d(hi, sel_bf16) + d(mid, sel_bf16) + d(lo, sel_bf16)


def _dot_f32(a, b):
    ah = a.astype(BF16)
    al = (a - ah.astype(F32)).astype(BF16)
    bh = b.astype(BF16)
    bl = (b - bh.astype(F32)).astype(BF16)
    d = functools.partial(jnp.dot, preferred_element_type=F32)
    return d(ah, bh) + d(al, bh) + d(ah, bl)


def _modulate(x, g, shift, scale):
    ms = jnp.mean(x * x, axis=-1, keepdims=True)
    return (x * lax.rsqrt(ms + RMS_EPS)) * g * (1.0 + scale) + shift


def _ada_kernel(c_ref, w_ref, b_ref, o_ref):
    a = _silu(c_ref[...]).astype(BF16)
    o_ref[...] = jnp.dot(a, w_ref[...].astype(BF16), preferred_element_type=F32) + b_ref[...]


def ada_all(c_rows, w_ada, b_ada, tn=1024):
    depth, d, n = w_ada.shape
    rows = c_rows.shape[0]
    return pl.pallas_call(
        _ada_kernel,
        grid=(depth, n // tn),
        in_specs=[
            pl.BlockSpec((rows, d), lambda l, j: (0, 0)),
            pl.BlockSpec((None, d, tn), lambda l, j: (l, 0, j)),
            pl.BlockSpec((None, 1, tn), lambda l, j: (l, 0, j)),
        ],
        out_specs=pl.BlockSpec((None, rows, tn), lambda l, j: (l, 0, j)),
        out_shape=jax.ShapeDtypeStruct((depth, rows, n), F32),
        compiler_params=_params("arbitrary", "arbitrary"),
        name="ada_all",
    )(c_rows, w_ada, b_ada.reshape(depth, 1, n))


def _mod_specs(mods, chunk_ids, tm, rows_per_batch):
    specs = []
    for c in chunk_ids:
        if mods.ndim == 3:
            specs.append(pl.BlockSpec((None, 1, D_MODEL),
                                      lambda i, j, c=c: ((i * tm) // rows_per_batch, 0, c)))
        else:
            specs.append(pl.BlockSpec((tm, D_MODEL), lambda i, j, c=c: (i, c)))
    return specs


def _gla_in_kernel(x_ref, g_ref, sh_ref, sc_ref, w_ref, wgd_ref, wup_ref, bg_ref, z_ref, lg_ref, h_scr):
    @pl.when(pl.program_id(1) == 0)
    def _():
        h = _modulate(x_ref[...], g_ref[...], sh_ref[...], sc_ref[...]).astype(BF16)
        h_scr[...] = h
        gd = jnp.dot(h, wgd_ref[...].astype(BF16), preferred_element_type=F32)
        gate = jnp.dot(gd.astype(BF16), wup_ref[...].astype(BF16), preferred_element_type=F32) + bg_ref[...]
        lg_ref[...] = _log_sigmoid(gate) * (1.0 / GLA_GATE_NORM)

    z_ref[...] = jnp.dot(h_scr[...], w_ref[...].astype(BF16), preferred_element_type=F32)


def gla_in_proj(x, gain, mods, w_in, w_gate_up, b_gate, tm, rows_per_batch, tn=512):
    m = x.shape[0]
    w_gd = w_in[:, GLA_MAIN:]
    sh_spec, sc_spec = _mod_specs(mods, (0, 1), tm, rows_per_batch)
    return pl.pallas_call(
        _gla_in_kernel,
        grid=(m // tm, GLA_MAIN // tn),
        in_specs=[
            pl.BlockSpec((tm, D_MODEL), lambda i, j: (i, 0)),
            pl.BlockSpec((1, D_MODEL), lambda i, j: (0, 0)),
            sh_spec, sc_spec,
            pl.BlockSpec((D_MODEL, tn), lambda i, j: (0, j)),
            pl.BlockSpec((D_MODEL, GLA_GATE_RANK), lambda i, j: (0, 0)),
            pl.BlockSpec((GLA_GATE_RANK, GLA_DK), lambda i, j: (0, 0)),
            pl.BlockSpec((1, GLA_DK), lambda i, j: (0, 0)),
        ],
        out_specs=[
            pl.BlockSpec((tm, tn), lambda i, j: (i, j)),
            pl.BlockSpec((tm, GLA_DK), lambda i, j: (i, 0)),
        ],
        out_shape=[jax.ShapeDtypeStruct((m, GLA_MAIN), F32), jax.ShapeDtypeStruct((m, GLA_DK), F32)],
        scratch_shapes=[pltpu.VMEM((tm, D_MODEL), BF16)],
        compiler_params=_params("arbitrary", "arbitrary"),
        name="gla_in_proj",
    )(x, gain.reshape(1, D_MODEL), mods, mods, w_in, w_gd, w_gate_up, b_gate.reshape(1, GLA_DK))


def _gla_scan_kernel(q_ref, k_ref, v_ref, lg_ref, s0_ref, o_ref, s_ref):
    C, R = GLA_CHUNK, GLA_SUB

    @pl.when(pl.program_id(2) == 0)
    def _():
        s_ref[...] = s0_ref[...]

    q = q_ref[...] * (GLA_DKH ** -0.5)
    k = k_ref[...]
    v = v_ref[...]
    vb = v.astype(BF16)
    row = lax.broadcasted_iota(jnp.int32, (C, C), 0)
    col = lax.broadcasted_iota(jnp.int32, (C, C), 1)
    tri = (row >= col).astype(BF16)
    b = _dot_exact_lhs(tri, lg_ref[...])
    state = s_ref[...]
    inter = jnp.dot((q * jnp.exp(b)).astype(BF16), state.astype(BF16), preferred_element_type=F32)

    t_idx = lax.broadcasted_iota(jnp.int32, (R, 1), 0)
    for i in range(C // R):
        lo = i * R
        bi = b[lo:lo + R]
        qi = q[lo:lo + R]
        ki = k[lo:lo + R]
        vi = v[lo:lo + R]
        oi = inter[lo:lo + R]
        if i > 0:
            b_ref_row = b[lo - 1:lo]
            qe = (qi * jnp.exp(bi - b_ref_row)).astype(BF16)
            ke = (k[:lo] * jnp.exp(b_ref_row - b[:lo])).astype(BF16)
            a = lax.dot_general(qe, ke, (((1,), (1,)), ((), ())), preferred_element_type=F32)
            oi = oi + jnp.dot(a.astype(BF16), vb[:lo], preferred_element_type=F32)
        for s in range(R):
            rel = jnp.where(t_idx >= s, bi - bi[s:s + 1], NEG_INF)
            w = jnp.sum(qi * ki[s:s + 1] * jnp.exp(rel), axis=-1, keepdims=True)
            oi = oi + w * vi[s:s + 1]
        o_ref[lo:lo + R, :] = oi

    b_t = b.T
    b_last = b_t[:, C - 1:C]
    ke_t = (k.T * jnp.exp(b_last - b_t)).astype(BF16)
    s_ref[...] = jnp.exp(b_last) * state + jnp.dot(ke_t, vb, preferred_element_type=F32)


def gla_scan(z, lg, s0):
    bsz, t, _ = z.shape
    C = GLA_CHUNK
    kq = GLA_DK // GLA_DKH
    return pl.pallas_call(
        _gla_scan_kernel,
        grid=(bsz, GLA_HEADS, t // C),
        in_specs=[
            pl.BlockSpec((None, C, GLA_DKH), lambda b, h, c: (b, c, h)),
            pl.BlockSpec((None, C, GLA_DKH), lambda b, h, c: (b, c, kq + h)),
            pl.BlockSpec((None, C, GLA_DVH), lambda b, h, c: (b, c, (2 * GLA_DK) // GLA_DVH + h)),
            pl.BlockSpec((None, C, GLA_DKH), lambda b, h, c: (b, c, h)),
            pl.BlockSpec((None, None, GLA_DKH, GLA_DVH), lambda b, h, c: (b, h, 0, 0)),
        ],
        out_specs=[
            pl.BlockSpec((None, C, GLA_DVH), lambda b, h, c: (b, c, h)),
            pl.BlockSpec((None, None, GLA_DKH, GLA_DVH), lambda b, h, c: (b, h, 0, 0)),
        ],
        out_shape=[jax.ShapeDtypeStruct((bsz, t, GLA_DV), F32),
                   jax.ShapeDtypeStruct((bsz, GLA_HEADS, GLA_DKH, GLA_DVH), F32)],
        compiler_params=_params("arbitrary", "arbitrary", "arbitrary"),
        name="gla_scan",
    )(z, z, z, lg, s0)


def _gla_out_kernel(o_ref, g_ref, ng_ref, w_ref, x_ref, ga_ref, y_ref, p_scr):
    @pl.when(pl.program_id(1) == 0)
    def _():
        gate = _silu(g_ref[...])
        for h in range(GLA_HEADS):
            sl = slice(h * GLA_DVH, (h + 1) * GLA_DVH)
            o = o_ref[:, sl]
            ms = jnp.mean(o * o, axis=-1, keepdims=True)
            p_scr[:, sl] = ((o * lax.rsqrt(ms + RMS_EPS)) * ng_ref[...] * gate[:, sl]).astype(BF16)

    y_ref[...] = x_ref[...] + ga_ref[...] * jnp.dot(p_scr[...], w_ref[...].astype(BF16),
                                                    preferred_element_type=F32)


def _fox_out_kernel(o_ref, g_ref, w_ref, x_ref, ga_ref, y_ref, p_scr):
    @pl.when(pl.program_id(1) == 0)
    def _():
        p_scr[...] = (o_ref[...] * jax.nn.sigmoid(g_ref[...])).astype(BF16)

    y_ref[...] = x_ref[...] + ga_ref[...] * jnp.dot(p_scr[...], w_ref[...].astype(BF16),
                                                    preferred_element_type=F32)


def out_proj(o, z, g_block, norm_gain, w_out, x, mods, tm, rows_per_batch, tn=512):
    m = x.shape[0]
    row_spec = pl.BlockSpec((tm, D_MODEL), lambda i, j: (i, 0))
    in_specs = [row_spec, pl.BlockSpec((tm, D_MODEL), lambda i, j: (i, g_block))]
    args = [o, z]
    if norm_gain is not None:
        in_specs.append(pl.BlockSpec((1, GLA_DVH), lambda i, j: (0, 0)))
        args.append(norm_gain.reshape(1, GLA_DVH))
        body = _gla_out_kernel
    else:
        body = _fox_out_kernel
    in_specs += [pl.BlockSpec((D_MODEL, tn), lambda i, j: (0, j)),
                 pl.BlockSpec((tm, tn), lambda i, j: (i, j))]
    args += [w_out, x]
    if mods.ndim == 3:
        ga_spec = pl.BlockSpec((None, 1, tn),
                               lambda i, j: ((i * tm) // rows_per_batch, 0, 2 * (D_MODEL // tn) + j))
    else:
        ga_spec = pl.BlockSpec((tm, tn), lambda i, j: (i, 2 * (D_MODEL // tn) + j))
    in_specs.append(ga_spec)
    args.append(mods)
    return pl.pallas_call(
        body,
        grid=(m // tm, D_MODEL // tn),
        in_specs=in_specs,
        out_specs=pl.BlockSpec((tm, tn), lambda i, j: (i, j)),
        out_shape=jax.ShapeDtypeStruct((m, D_MODEL), F32),
        scratch_shapes=[pltpu.VMEM((tm, D_MODEL), BF16)],
        compiler_params=_params("arbitrary", "arbitrary"),
        name="out_proj",
    )(*args)


def _fox_in_kernel(x_ref, g_ref, sh_ref, sc_ref, w_ref, wf_ref, bf_ref, qn_ref, kn_ref, z_ref, lf_ref, h_scr,
                   *, tn):
    j = pl.program_id(1)

    @pl.when(j == 0)
    def _():
        h = _modulate(x_ref[...], g_ref[...], sh_ref[...], sc_ref[...]).astype(BF16)
        h_scr[...] = h
        f = jnp.dot(h, wf_ref[...].astype(BF16), preferred_element_type=F32)
        lf_ref[...] = _log_sigmoid(f + bf_ref[...])

    acc = jnp.dot(h_scr[...], w_ref[...].astype(BF16), preferred_element_type=F32)
    qk_tiles = FOX_DIM // tn

    @pl.when(j < 2 * qk_tiles)
    def _():
        gain = jnp.where(j < qk_tiles, qn_ref[...], kn_ref[...])
        for c in range(tn // FOX_HD):
            blk = acc[:, c * FOX_HD:(c + 1) * FOX_HD]
            ms = jnp.mean(blk * blk, axis=-1, keepdims=True)
            z_ref[:, c * FOX_HD:(c + 1) * FOX_HD] = (blk * lax.rsqrt(ms + RMS_EPS)) * gain

    @pl.when(j >= 2 * qk_tiles)
    def _():
        z_ref[...] = acc


def fox_in_proj(x, gain, mods, w_in, b_f, q_norm, k_norm, tm, rows_per_batch, tn=512):
    m = x.shape[0]
    n_main = 4 * FOX_DIM
    w_f = w_in[:, n_main:]
    sh_spec, sc_spec = _mod_specs(mods, (0, 1), tm, rows_per_batch)
    return pl.pallas_call(
        functools.partial(_fox_in_kernel, tn=tn),
        grid=(m // tm, n_main // tn),
        in_specs=[
            pl.BlockSpec((tm, D_MODEL), lambda i, j: (i, 0)),
            pl.BlockSpec((1, D_MODEL), lambda i, j: (0, 0)),
            sh_spec, sc_spec,
            pl.BlockSpec((D_MODEL, tn), lambda i, j: (0, j)),
            pl.BlockSpec((D_MODEL, FOX_HEADS), lambda i, j: (0, 0)),
            pl.BlockSpec((1, FOX_HEADS), lambda i, j: (0, 0)),
            pl.BlockSpec((1, FOX_HD), lambda i, j: (0, 0)),
            pl.BlockSpec((1, FOX_HD), lambda i, j: (0, 0)),
        ],
        out_specs=[
            pl.BlockSpec((tm, tn), lambda i, j: (i, j)),
            pl.BlockSpec((tm, FOX_HEADS), lambda i, j: (i, 0)),
        ],
        out_shape=[jax.ShapeDtypeStruct((m, n_main), F32), jax.ShapeDtypeStruct((m, FOX_HEADS), F32)],
        scratch_shapes=[pltpu.VMEM((tm, D_MODEL), BF16)],
        compiler_params=_params("arbitrary", "arbitrary"),
        name="fox_in_proj",
    )(x, gain.reshape(1, D_MODEL), mods, mods, w_in, w_f, b_f.reshape(1, FOX_HEADS),
      q_norm.reshape(1, FOX_HD), k_norm.reshape(1, FOX_HD))


def _cumsum_kernel(x_ref, o_ref):
    t = x_ref.shape[0]
    blk = LANES
    row = lax.broadcasted_iota(jnp.int32, (blk, blk), 0)
    col = lax.broadcasted_iota(jnp.int32, (blk, blk), 1)
    tri = (row >= col).astype(BF16)
    carry = jnp.zeros((1, x_ref.shape[1]), F32)
    for i in range(t // blk):
        c = _dot_exact_lhs(tri, x_ref[i * blk:(i + 1) * blk, :]) + carry
        o_ref[i * blk:(i + 1) * blk, :] = c
        carry = c[blk - 1:blk]


def cumsum_time(x):
    bsz, t, h = x.shape
    return pl.pallas_call(
        _cumsum_kernel,
        grid=(bsz,),
        in_specs=[pl.BlockSpec((None, t, h), lambda b: (b, 0, 0))],
        out_specs=pl.BlockSpec((None, t, h), lambda b: (b, 0, 0)),
        out_shape=jax.ShapeDtypeStruct((bsz, t, h), F32),
        compiler_params=_params("arbitrary"),
        name="cumsum_time",
    )(x)


def _fox_attn_kernel(q_ref, k_ref, v_ref, cc_ref, cr_ref, o_ref):
    blk = ATTN_BLOCK
    i = pl.program_id(2)
    q = (q_ref[...] * (FOX_HD ** -0.5)).astype(BF16)
    cq = cc_ref[...]

    def scores(j):
        off = pl.multiple_of(j * blk, blk)
        kj = k_ref[pl.ds(off, blk), :].astype(BF16)
        vj = v_ref[pl.ds(off, blk), :].astype(BF16)
        s = lax.dot_general(q, kj, (((1,), (1,)), ((), ())), preferred_element_type=F32)
        return s + cq - cr_ref[j], vj

    def update(carry, s, vj):
        m, l, acc = carry
        m_new = jnp.maximum(m, jnp.max(s, axis=-1, keepdims=True))
        alpha = jnp.exp(m - m_new)
        p = jnp.exp(s - m_new)
        l = alpha * l + jnp.sum(p, axis=-1, keepdims=True)
        acc = alpha * acc + jnp.dot(p.astype(BF16), vj, preferred_element_type=F32)
        return m_new, l, acc

    def body(j, carry):
        s, vj = scores(j)
        return update(carry, s, vj)

    carry = (jnp.full((blk, 1), NEG_INF, F32), jnp.zeros((blk, 1), F32), jnp.zeros((blk, FOX_HD), F32))
    carry = lax.fori_loop(0, i, body, carry)
    s, vj = scores(i)
    row = lax.broadcasted_iota(jnp.int32, (blk, blk), 0)
    col = lax.broadcasted_iota(jnp.int32, (blk, blk), 1)
    _, l, acc = update(carry, jnp.where(col <= row, s, NEG_INF), vj)
    o_ref[...] = acc / l


def fox_prompt_attn(z, cum):
    bsz, t, _ = z.shape
    blk = ATTN_BLOCK
    nb = t // blk
    cum_h = jnp.transpose(cum, (0, 2, 1))
    cum_col = cum_h.reshape(bsz, FOX_HEADS, t, 1)
    cum_row = cum_h.reshape(bsz, FOX_HEADS, nb, 1, blk)
    return pl.pallas_call(
        _fox_attn_kernel,
        grid=(bsz, FOX_HEADS, nb),
        in_specs=[
            pl.BlockSpec((None, blk, FOX_HD), lambda b, h, i: (b, i, h)),
            pl.BlockSpec((None, t, FOX_HD), lambda b, h, i: (b, 0, FOX_HEADS + h)),
            pl.BlockSpec((None, t, FOX_HD), lambda b, h, i: (b, 0, 2 * FOX_HEADS + h)),
            pl.BlockSpec((None, None, blk, 1), lambda b, h, i: (b, h, i, 0)),
            pl.BlockSpec((None, None, nb, 1, blk), lambda b, h, i: (b, h, 0, 0, 0)),
        ],
        out_specs=pl.BlockSpec((None, blk, FOX_HD), lambda b, h, i: (b, i, h)),
        out_shape=jax.ShapeDtypeStruct((bsz, t, FOX_DIM), F32),
        compiler_params=_params("arbitrary", "arbitrary", "arbitrary"),
        name="fox_prompt_attn",
    )(z, z, z, cum_col, cum_row)


def _fox_paged_kernel(pt_ref, wt_ref, ct_ref, kn_ref, vn_ref, bn_ref, *rest, s_len):
    G = PAGES_PER_STEP
    lf_refs = rest[0:G]
    k_refs = rest[G:2 * G]
    v_refs = rest[2 * G:3 * G]
    o_ref = rest[3 * G]
    m_scr, l_scr, acc_scr, carry_scr = rest[3 * G + 1:]
    t = pl.program_id(1)
    page = k_refs[0].shape[0]
    nt_dims = (((1,), (1,)), ((), ()))

    @pl.when(t == 0)
    def _():
        m_scr[...] = jnp.full(m_scr.shape, NEG_INF, F32)
        l_scr[...] = jnp.zeros(l_scr.shape, F32)
        acc_scr[...] = jnp.zeros(acc_scr.shape, F32)
        carry_scr[...] = jnp.zeros(carry_scr.shape, F32)

    wt = wt_ref[...]
    row = lax.broadcasted_iota(jnp.int32, (page, page), 0)
    col = lax.broadcasted_iota(jnp.int32, (page, page), 1)
    upper = (col > row).astype(BF16)
    hh = lax.broadcasted_iota(jnp.int32, (wt.shape[0], FOX_HEADS), 1)
    cc = lax.broadcasted_iota(jnp.int32, (wt.shape[0], FOX_HEADS), 0)
    expand_t = ((cc // s_len == hh) & (cc < s_len * FOX_HEADS)).astype(BF16)

    def bias_t(suf):
        hi, mid, lo = _split3(suf)
        d = lambda x: lax.dot_general(expand_t, x, nt_dims, preferred_element_type=F32)
        return d(hi) + d(mid) + d(lo)

    def online(s, v_list):
        m = m_scr[...]
        m_new = jnp.maximum(m, jnp.max(s, axis=-1, keepdims=True))
        alpha = jnp.exp(m - m_new)
        p = jnp.exp(s - m_new)
        l_scr[...] = alpha * l_scr[...] + jnp.sum(p, axis=-1, keepdims=True)
        pv = None
        off = 0
        for vv in v_list:
            n = vv.shape[0]
            term = jnp.dot(p[:, off:off + n].astype(BF16), vv.astype(BF16), preferred_element_type=F32)
            pv = term if pv is None else pv + term
            off += n
        acc_scr[...] = alpha * acc_scr[...] + pv
        m_scr[...] = m_new

    carry = carry_scr[...]
    s_parts = []
    for g in range(G):
        lf = lf_refs[g][...]
        suf = _dot_exact_lhs(upper, lf) + carry
        carry = carry + jnp.sum(lf, axis=0, keepdims=True)
        s = lax.dot_general(wt, k_refs[g][...].astype(BF16), nt_dims, preferred_element_type=F32)
        s_parts.append(s + bias_t(suf))
    carry_scr[...] = carry
    online(jnp.concatenate(s_parts, axis=1) + ct_ref[...], [r[...] for r in v_refs])

    @pl.when(t == pl.num_programs(1) - 1)
    def _():
        s = lax.dot_general(wt, kn_ref[...].astype(BF16), nt_dims, preferred_element_type=F32)
        online(s + bn_ref[...], [vn_ref[...]])
        o_ref[...] = acc_scr[...] / l_scr[...]


def fox_paged_attn(q, k_new, v_new, logf_new, cache_k, cache_v, cache_logf, layer, page_table):
    db, s_len, _, _ = q.shape
    n_pages = page_table.shape[1]
    page = cache_k.shape[2]
    G = PAGES_PER_STEP
    cols = LANES
    new_rows = 16
    n_fox, n_pool = cache_k.shape[:2]
    ck = cache_k.reshape(n_fox, n_pool, page, FOX_DIM)
    cv = cache_v.reshape(n_fox, n_pool, page, FOX_DIM)

    eye = jnp.eye(FOX_HEADS, dtype=F32)
    qs = q * (FOX_HD ** -0.5)
    wt = jnp.einsum('brhd,hg->bhrgd', qs, eye).reshape(db, FOX_HEADS * s_len, FOX_DIM)
    wt = jnp.pad(wt, ((0, 0), (0, cols - FOX_HEADS * s_len), (0, 0))).astype(BF16)
    cum = jnp.cumsum(logf_new, axis=1)
    cum_hr = jnp.transpose(cum, (0, 2, 1))
    ct = jnp.pad(cum_hr.reshape(db, FOX_HEADS * s_len, 1), ((0, 0), (0, cols - FOX_HEADS * s_len), (0, 0)))
    causal = jnp.arange(s_len)[None, :] <= jnp.arange(s_len)[:, None]
    bn = jnp.where(causal[None, None], cum_hr[:, :, :, None] - cum_hr[:, :, None, :], NEG_INF)
    bn = jnp.pad(bn.reshape(db, FOX_HEADS * s_len, s_len),
                 ((0, 0), (0, cols - FOX_HEADS * s_len), (0, new_rows - s_len)), constant_values=NEG_INF)
    kn = jnp.pad(k_new.reshape(db, s_len, FOX_DIM), ((0, 0), (0, new_rows - s_len), (0, 0)))
    vn = jnp.pad(v_new.reshape(db, s_len, FOX_DIM), ((0, 0), (0, new_rows - s_len), (0, 0)))

    def page_idx(b, t, pt, g):
        return pt[b, n_pages - 1 - (t * G + g)]

    lf_specs = [pl.BlockSpec((None, None, page, FOX_HEADS),
                             lambda b, t, pt, g=g: (layer, page_idx(b, t, pt, g), 0, 0)) for g in range(G)]
    kv_specs = [pl.BlockSpec((None, None, page, FOX_DIM),
                             lambda b, t, pt, g=g: (layer, page_idx(b, t, pt, g), 0, 0)) for g in range(G)]
    grid_spec = pltpu.PrefetchScalarGridSpec(
        num_scalar_prefetch=1,
        grid=(db, n_pages // G),
        in_specs=[
            pl.BlockSpec((None, cols, FOX_DIM), lambda b, t, pt: (b, 0, 0)),
            pl.BlockSpec((None, cols, 1), lambda b, t, pt: (b, 0, 0)),
            pl.BlockSpec((None, new_rows, FOX_DIM), lambda b, t, pt: (b, 0, 0)),
            pl.BlockSpec((None, new_rows, FOX_DIM), lambda b, t, pt: (b, 0, 0)),
            pl.BlockSpec((None, cols, new_rows), lambda b, t, pt: (b, 0, 0)),
        ] + lf_specs + kv_specs + kv_specs,
        out_specs=pl.BlockSpec((None, cols, FOX_DIM), lambda b, t, pt: (b, 0, 0)),
        scratch_shapes=[pltpu.VMEM((cols, 1), F32), pltpu.VMEM((cols, 1), F32),
                        pltpu.VMEM((cols, FOX_DIM), F32), pltpu.VMEM((1, FOX_HEADS), F32)],
    )
    o_full = pl.pallas_call(
        functools.partial(_fox_paged_kernel, s_len=s_len),
        grid_spec=grid_spec,
        out_shape=jax.ShapeDtypeStruct((db, cols, FOX_DIM), F32),
        compiler_params=_params("arbitrary", "arbitrary"),
        name="fox_paged_attn",
    )(page_table, wt, ct, kn, vn, bn, *([cache_logf] * G), *([ck] * G), *([cv] * G))
    o5 = o_full[:, :FOX_HEADS * s_len].reshape(db, FOX_HEADS, s_len, FOX_HEADS, FOX_HD)
    o = jnp.diagonal(o5, axis1=1, axis2=3)
    return jnp.transpose(o, (0, 1, 3, 2))


def _moe_pre_kernel(x_ref, g_ref, sh_ref, sc_ref, wr_ref, br_ref, h_ref, lo_ref):
    h = _modulate(x_ref[...], g_ref[...], sh_ref[...], sc_ref[...])
    h_ref[...] = h.astype(BF16)
    lo_ref[...] = _dot_f32(h, wr_ref[...]) + br_ref[...]


def moe_pre(x, gain, mods, w_router, b_router, tm, rows_per_batch):
    m = x.shape[0]
    sh_spec, sc_spec = _mod_specs(mods, (3, 4), tm, rows_per_batch)
    return pl.pallas_call(
        _moe_pre_kernel,
        grid=(m // tm, 1),
        in_specs=[
            pl.BlockSpec((tm, D_MODEL), lambda i, j: (i, 0)),
            pl.BlockSpec((1, D_MODEL), lambda i, j: (0, 0)),
            sh_spec, sc_spec,
            pl.BlockSpec((D_MODEL, ROUTER_COLS), lambda i, j: (0, 0)),
            pl.BlockSpec((1, ROUTER_COLS), lambda i, j: (0, 0)),
        ],
        out_specs=[
            pl.BlockSpec((tm, D_MODEL), lambda i, j: (i, 0)),
            pl.BlockSpec((tm, ROUTER_COLS), lambda i, j: (i, 0)),
        ],
        out_shape=[jax.ShapeDtypeStruct((m, D_MODEL), BF16), jax.ShapeDtypeStruct((m, ROUTER_COLS), F32)],
        compiler_params=_params("arbitrary", "arbitrary"),
        name="moe_pre",
    )(x, gain.reshape(1, D_MODEL), mods, mods, w_router, b_router)


def _moe_expert_kernel(be_ref, nu_ref, x_ref, wg_ref, wu_ref, wd_ref, y_ref, wg_s, wu_s, wd_s):
    i = pl.program_id(0)
    prev = be_ref[jnp.maximum(i - 1, 0)]

    @pl.when((i == 0) | (be_ref[i] != prev))
    def _():
        wg_s[...] = wg_ref[...].astype(BF16)
        wu_s[...] = wu_ref[...].astype(BF16)
        wd_s[...] = wd_ref[...].astype(BF16)

    @pl.when(i < nu_ref[0])
    def _():
        x = x_ref[...]
        a = jnp.dot(x, wg_s[...], preferred_element_type=F32)
        u = jnp.dot(x, wu_s[...], preferred_element_type=F32)
        y_ref[...] = jnp.dot((_silu(a) * u).astype(BF16), wd_s[...], preferred_element_type=F32)

    @pl.when(i >= nu_ref[0])
    def _():
        y_ref[...] = jnp.zeros(y_ref.shape, F32)


def moe_experts(xs, block_e, n_used, w_gate, w_up, w_down, layer):
    cap = xs.shape[0]
    tb = MOE_BLOCK
    grid_spec = pltpu.PrefetchScalarGridSpec(
        num_scalar_prefetch=2,
        grid=(cap // tb,),
        in_specs=[
            pl.BlockSpec((tb, D_MODEL), lambda i, be, nu: (i, 0)),
            pl.BlockSpec((None, None, D_MODEL, MOE_DFF), lambda i, be, nu: (layer, be[i], 0, 0)),
            pl.BlockSpec((None, None, D_MODEL, MOE_DFF), lambda i, be, nu: (layer, be[i], 0, 0)),
            pl.BlockSpec((None, None, MOE_DFF, D_MODEL), lambda i, be, nu: (layer, be[i], 0, 0)),
        ],
        out_specs=pl.BlockSpec((tb, D_MODEL), lambda i, be, nu: (i, 0)),
        scratch_shapes=[pltpu.VMEM((D_MODEL, MOE_DFF), BF16), pltpu.VMEM((D_MODEL, MOE_DFF), BF16),
                        pltpu.VMEM((MOE_DFF, D_MODEL), BF16)],
    )
    return pl.pallas_call(
        _moe_expert_kernel,
        grid_spec=grid_spec,
        out_shape=jax.ShapeDtypeStruct((cap, D_MODEL), F32),
        compiler_params=_params("arbitrary"),
        name="moe_experts",
    )(block_e, n_used, xs, w_gate, w_up, w_down)


def _route(logits):
    n = logits.shape[0]
    pg = jax.nn.softmax(logits[:, :MOE_GROUPS], axis=-1)
    g_idx = jnp.argmax(pg, axis=-1)
    p_g = jnp.max(pg, axis=-1)
    le = logits[:, MOE_GROUPS:MOE_GROUPS + MOE_EXPERTS].reshape(n, MOE_GROUPS, MOE_PER_GROUP)
    le_g = jnp.take_along_axis(le, g_idx[:, None, None], axis=1)[:, 0]
    pe = jax.nn.softmax(le_g, axis=-1)
    top_w, top_i = lax.top_k(pe, MOE_TOPK)
    top_w = top_w / jnp.sum(top_w, axis=-1, keepdims=True)
    return (g_idx[:, None] * MOE_PER_GROUP + top_i).astype(jnp.int32), p_g[:, None] * top_w


def _dispatch(e_ids):
    n = e_ids.shape[0]
    tb = MOE_BLOCK
    e_flat = e_ids.reshape(-1)
    n_assign = e_flat.shape[0]
    onehot = (e_flat[:, None] == jnp.arange(MOE_EXPERTS, dtype=jnp.int32)[None, :]).astype(jnp.int32)
    rank = jnp.sum((jnp.cumsum(onehot, axis=0) - onehot) * onehot, axis=1)
    counts = jnp.sum(onehot, axis=0)
    padded = ((counts + tb - 1) // tb) * tb
    pend = jnp.cumsum(padded)
    pstart = pend - padded
    dest = pstart[e_flat] + rank
    n_blocks = (n_assign + MOE_EXPERTS * (tb - 1) + tb - 1) // tb
    tok = jnp.repeat(jnp.arange(n, dtype=jnp.int32), MOE_TOPK)
    buf_tok = jnp.zeros((n_blocks * tb,), jnp.int32).at[dest].set(tok)
    block_e = jnp.minimum(jnp.searchsorted(pend, jnp.arange(n_blocks, dtype=jnp.int32) * tb, side='right'),
                          MOE_EXPERTS - 1).astype(jnp.int32)
    n_used = (pend[-1] // tb).astype(jnp.int32).reshape(1)
    return dest.reshape(n, MOE_TOPK), buf_tok, block_e, n_used


def _combine_kernel(x_ref, ga_ref, y0_ref, y1_ref, w_ref, o_ref):
    w = w_ref[...]
    moe = y0_ref[...] * w[:, 0:1] + y1_ref[...] * w[:, 1:2]
    o_ref[...] = x_ref[...] + ga_ref[...] * moe


def moe_combine(x, mods, y0, y1, gate_w, tm, rows_per_batch):
    m = x.shape[0]
    (ga_spec,) = _mod_specs(mods, (5,), tm, rows_per_batch)
    row_spec = pl.BlockSpec((tm, D_MODEL), lambda i, j: (i, 0))
    return pl.pallas_call(
        _combine_kernel,
        grid=(m // tm, 1),
        in_specs=[row_spec, ga_spec, row_spec, row_spec, pl.BlockSpec((tm, MOE_TOPK), lambda i, j: (i, 0))],
        out_specs=row_spec,
        out_shape=jax.ShapeDtypeStruct((m, D_MODEL), F32),
        compiler_params=_params("arbitrary", "arbitrary"),
        name="moe_combine",
    )(x, mods, y0, y1, gate_w)


def _final_norm_kernel(x_ref, g_ref, o_ref):
    x = x_ref[...]
    ms = jnp.mean(x * x, axis=-1, keepdims=True)
    o_ref[...] = (x * lax.rsqrt(ms + RMS_EPS)) * g_ref[...]


def final_norm(x, gain, tm):
    m = x.shape[0]
    row_spec = pl.BlockSpec((tm, D_MODEL), lambda i: (i, 0))
    return pl.pallas_call(
        _final_norm_kernel,
        grid=(m // tm,),
        in_specs=[row_spec, pl.BlockSpec((1, D_MODEL), lambda i: (0, 0))],
        out_specs=row_spec,
        out_shape=jax.ShapeDtypeStruct((m, D_MODEL), F32),
        compiler_params=_params("arbitrary"),
        name="final_norm",
    )(x, gain.reshape(1, D_MODEL))


def kernel(x_prompt, x_sample, c_prompt, c_sample, state_gla, cache_k, cache_v, cache_logf, page_table, norm_mix, norm_ffn, norm_final, w_ada, b_ada, gla_w_in, gla_w_gate_up, gla_b_gate, gla_norm, gla_w_out, fox_w_in, fox_b_f, fox_q_norm, fox_k_norm, fox_w_out, moe_w_group, moe_b_group, moe_w_expert, moe_b_expert, moe_w_gate, moe_w_up, moe_w_down):
    bsz, seq, d = x_prompt.shape
    db, ds, _ = x_sample.shape
    depth = w_ada.shape[0]
    mp, msz = bsz * seq, db * ds
    tm_p = min(512, seq)
    te_p = min(256, seq)
    assert d == D_MODEL and seq % tm_p == 0 and seq % GLA_CHUNK == 0 and seq % ATTN_BLOCK == 0
    assert ds <= GLA_CHUNK and page_table.shape[1] % PAGES_PER_STEP == 0

    c_rows = jnp.concatenate([c_prompt, c_sample], axis=0)
    c_rows = jnp.pad(c_rows, ((0, (-c_rows.shape[0]) % 8), (0, 0)))
    ada = ada_all(c_rows, w_ada, b_ada)

    xp = x_prompt.reshape(mp, d)
    xs = x_sample.reshape(msz, d)
    gla_p, gla_s, kp_l, vp_l, lfp_l, ks_l, vs_l, lfs_l = [], [], [], [], [], [], [], []
    for i in range(depth):
        j = i // 2
        mods_p = ada[i, :bsz].reshape(bsz, 1, 6 * d)
        mods_s = jnp.repeat(ada[i, bsz:bsz + db], ds, axis=0)
        if i % 2 == 0:
            zp, lgp = gla_in_proj(xp, norm_mix[i], mods_p, gla_w_in[j], gla_w_gate_up[j], gla_b_gate[j],
                                  tm_p, seq)
            s0 = jnp.zeros((bsz, GLA_HEADS, GLA_DKH, GLA_DVH), F32)
            op, s_fin = gla_scan(zp.reshape(bsz, seq, GLA_MAIN), lgp.reshape(bsz, seq, GLA_DK), s0)
            xp = out_proj(op.reshape(mp, d), zp, 2, gla_norm[j], gla_w_out[j], xp, mods_p, tm_p, seq)
            gla_p.append(s_fin)

            zs, lgs = gla_in_proj(xs, norm_mix[i], mods_s, gla_w_in[j], gla_w_gate_up[j], gla_b_gate[j],
                                  msz, msz)
            pad = ((0, 0), (0, GLA_CHUNK - ds), (0, 0))
            zs_pad = jnp.pad(zs.reshape(db, ds, GLA_MAIN), pad)
            lgs_pad = jnp.pad(lgs.reshape(db, ds, GLA_DK), pad)
            os_pad, s_new = gla_scan(zs_pad, lgs_pad, state_gla[j])
            xs = out_proj(os_pad[:, :ds].reshape(msz, d), zs, 2, gla_norm[j], gla_w_out[j], xs, mods_s,
                          msz, msz)
            gla_s.append(s_new)
        else:
            zp, lfp = fox_in_proj(xp, norm_mix[i], mods_p, fox_w_in[j], fox_b_f[j], fox_q_norm[j],
                                  fox_k_norm[j], tm_p, seq)
            z3 = zp.reshape(bsz, seq, 4 * FOX_DIM)
            lf3 = lfp.reshape(bsz, seq, FOX_HEADS)
            op = fox_prompt_attn(z3, cumsum_time(lf3))
            xp = out_proj(op.reshape(mp, d), zp, 3, None, fox_w_out[j], xp, mods_p, tm_p, seq)
            kp_l.append(z3[:, :, FOX_DIM:2 * FOX_DIM].reshape(bsz, seq, FOX_HEADS, FOX_HD))
            vp_l.append(z3[:, :, 2 * FOX_DIM:3 * FOX_DIM].reshape(bsz, seq, FOX_HEADS, FOX_HD))
            lfp_l.append(lf3)

            zs, lfs = fox_in_proj(xs, norm_mix[i], mods_s, fox_w_in[j], fox_b_f[j], fox_q_norm[j],
                                  fox_k_norm[j], msz, msz)
            zs4 = zs.reshape(db, ds, 4, FOX_HEADS, FOX_HD)
            q_s, k_s, v_s = zs4[:, :, 0], zs4[:, :, 1], zs4[:, :, 2]
            lfs3 = lfs.reshape(db, ds, FOX_HEADS)
            os_ = fox_paged_attn(q_s, k_s, v_s, lfs3, cache_k, cache_v, cache_logf, j, page_table)
            xs = out_proj(os_.reshape(msz, d), zs, 3, None, fox_w_out[j], xs, mods_s, msz, msz)
            ks_l.append(k_s)
            vs_l.append(v_s)
            lfs_l.append(lfs3)

        w_router = jnp.pad(jnp.concatenate([moe_w_group[i], moe_w_expert[i]], axis=1),
                           ((0, 0), (0, ROUTER_COLS - MOE_GROUPS - MOE_EXPERTS)))
        b_router = jnp.pad(jnp.concatenate([moe_b_group[i], moe_b_expert[i]]),
                           (0, ROUTER_COLS - MOE_GROUPS - MOE_EXPERTS)).reshape(1, ROUTER_COLS)
        hp, lop = moe_pre(xp, norm_ffn[i], mods_p, w_router, b_router, tm_p, seq)
        hs, los = moe_pre(xs, norm_ffn[i], mods_s, w_router, b_router, msz, msz)
        h_all = jnp.concatenate([hp, hs], axis=0)
        e_ids, gate_w = _route(jnp.concatenate([lop, los], axis=0))
        slot, buf_tok, block_e, n_used = _dispatch(e_ids)
        y = moe_experts(h_all[buf_tok], block_e, n_used, moe_w_gate, moe_w_up, moe_w_down, i)
        y0, y1 = y[slot[:, 0]], y[slot[:, 1]]
        xp = moe_combine(xp, mods_p, y0[:mp], y1[:mp], gate_w[:mp], te_p, seq)
        xs = moe_combine(xs, mods_s, y0[mp:], y1[mp:], gate_w[mp:], msz, msz)

    y_prompt = final_norm(xp, norm_final, te_p).reshape(bsz, seq, d)
    y_sample = final_norm(xs, norm_final, msz).reshape(db, ds, d)
    return (y_prompt, y_sample,
            jnp.stack(kp_l), jnp.stack(vp_l), jnp.stack(lfp_l), jnp.stack(gla_p),
            jnp.stack(ks_l), jnp.stack(vs_l), jnp.stack(lfs_l), jnp.stack(gla_s))
```

```python
import functools

import jax
import jax.numpy as jnp
from jax import lax
from jax.experimental import pallas as pl
from jax.experimental.pallas import tpu as pltpu

F32 = jnp.float32
BF16 = jnp.bfloat16

D_MODEL = 2048
GLA_HEADS = 4
GLA_DK = D_MODEL // 2
GLA_DV = D_MODEL
GLA_DKH = GLA_DK // GLA_HEADS
GLA_DVH = GLA_DV // GLA_HEADS
GLA_GATE_RANK = 16
GLA_GATE_NORM = 16.0
GLA_MAIN = 2 * GLA_DK + 2 * GLA_DV
FOX_HEADS = 16
FOX_HD = D_MODEL // FOX_HEADS
FOX_DIM = FOX_HEADS * FOX_HD
MOE_GROUPS = 4
MOE_PER_GROUP = 8
MOE_EXPERTS = MOE_GROUPS * MOE_PER_GROUP
MOE_TOPK = 2
MOE_DFF = D_MODEL // 4
RMS_EPS = 1e-6
NEG_INF = -1e30

VMEM_LIMIT_BYTES = 52 * 1024 * 1024
LANES = 128

GLA_CHUNK = 128
GLA_SUB = 16
ATTN_BLOCK = 256
PAGES_PER_STEP = 4
SUFFIX_PAGES_PER_STEP = 16
ROUTE_BLOCK = 256
MOE_BLOCK = 256
ROUTER_COLS = 128


def _params(*sem):
    return pltpu.CompilerParams(dimension_semantics=sem, vmem_limit_bytes=VMEM_LIMIT_BYTES)


def _log_sigmoid(x):
    return jnp.minimum(x, 0.0) - jnp.log1p(jnp.exp(-jnp.abs(x)))


def _silu(x):
    return x * jax.nn.sigmoid(x)


def _split3(a):
    hi = a.astype(BF16)
    r1 = a - hi.astype(F32)
    mid = r1.astype(BF16)
    lo = (r1 - mid.astype(F32)).astype(BF16)
    return hi, mid, lo


def _dot_exact_lhs(sel_bf16, x_f32):
    hi, mid, lo = _split3(x_f32)
    d = functools.partial(jnp.dot, preferred_element_type=F32)
    return d(sel_bf16, hi) + d(sel_bf16, mid) + d(sel_bf16, lo)


def _modulate(x, g, shift, scale):
    ms = jnp.mean(x * x, axis=-1, keepdims=True)
    return (x * lax.rsqrt(ms + RMS_EPS)) * g * (1.0 + scale) + shift


def _ada_kernel(c_ref, w_ref, b_ref, o_ref):
    a = _silu(c_ref[...]).astype(BF16)
    o_ref[...] = jnp.dot(a, w_ref[...].astype(BF16), preferred_element_type=F32) + b_ref[...]


def ada_all(c_rows, w_ada, b_ada, tn=1024):
    depth, d, n = w_ada.shape
    rows = c_rows.shape[0]
    return pl.pallas_call(
        _ada_kernel,
        grid=(depth, n // tn),
        in_specs=[
            pl.BlockSpec((rows, d), lambda l, j: (0, 0)),
            pl.BlockSpec((None, d, tn), lambda l, j: (l, 0, j)),
            pl.BlockSpec((None, 1, tn), lambda l, j: (l, 0, j)),
        ],
        out_specs=pl.BlockSpec((None, rows, tn), lambda l, j: (l, 0, j)),
        out_shape=jax.ShapeDtypeStruct((depth, rows, n), F32),
        compiler_params=_params("arbitrary", "arbitrary"),
        name="ada_all",
    )(c_rows, w_ada, b_ada.reshape(depth, 1, n))


def _mod_specs(mods, chunk_ids, tm, rows_per_batch):
    specs = []
    for c in chunk_ids:
        if mods.ndim == 3:
            specs.append(pl.BlockSpec((None, 1, D_MODEL),
                                      lambda i, j, c=c: ((i * tm) // rows_per_batch, 0, c)))
        else:
            specs.append(pl.BlockSpec((tm, D_MODEL), lambda i, j, c=c: (i, c)))
    return specs


def _gla_in_kernel(x_ref, g_ref, sh_ref, sc_ref, w_ref, wgd_ref, wup_ref, bg_ref, z_ref, lg_ref, h_scr):
    @pl.when(pl.program_id(1) == 0)
    def _():
        h = _modulate(x_ref[...], g_ref[...], sh_ref[...], sc_ref[...]).astype(BF16)
        h_scr[...] = h
        gd = jnp.dot(h, wgd_ref[...], preferred_element_type=F32)
        gate = jnp.dot(gd.astype(BF16), wup_ref[...], preferred_element_type=F32) + bg_ref[...]
        lg_ref[...] = _log_sigmoid(gate) * (1.0 / GLA_GATE_NORM)

    z_ref[...] = jnp.dot(h_scr[...], w_ref[...], preferred_element_type=F32)


def gla_in_proj(x, gain, mods, w_in, w_gate_up, b_gate, tm, rows_per_batch, tn=512):
    m = x.shape[0]
    w_gd = w_in[:, GLA_MAIN:]
    sh_spec, sc_spec = _mod_specs(mods, (0, 1), tm, rows_per_batch)
    return pl.pallas_call(
        _gla_in_kernel,
        grid=(m // tm, GLA_MAIN // tn),
        in_specs=[
            pl.BlockSpec((tm, D_MODEL), lambda i, j: (i, 0)),
            pl.BlockSpec((1, D_MODEL), lambda i, j: (0, 0)),
            sh_spec, sc_spec,
            pl.BlockSpec((D_MODEL, tn), lambda i, j: (0, j)),
            pl.BlockSpec((D_MODEL, GLA_GATE_RANK), lambda i, j: (0, 0)),
            pl.BlockSpec((GLA_GATE_RANK, GLA_DK), lambda i, j: (0, 0)),
            pl.BlockSpec((1, GLA_DK), lambda i, j: (0, 0)),
        ],
        out_specs=[
            pl.BlockSpec((tm, tn), lambda i, j: (i, j)),
            pl.BlockSpec((tm, GLA_DK), lambda i, j: (i, 0)),
        ],
        out_shape=[jax.ShapeDtypeStruct((m, GLA_MAIN), F32), jax.ShapeDtypeStruct((m, GLA_DK), F32)],
        scratch_shapes=[pltpu.VMEM((tm, D_MODEL), BF16)],
        compiler_params=_params("arbitrary", "arbitrary"),
        name="gla_in_proj",
    )(x, gain.reshape(1, D_MODEL), mods, mods, w_in, w_gd, w_gate_up, b_gate.reshape(1, GLA_DK))


def _gla_scan_kernel(q_ref, k_ref, v_ref, lg_ref, s0_ref, o_ref, s_ref):
    C, R = GLA_CHUNK, GLA_SUB

    @pl.when(pl.program_id(2) == 0)
    def _():
        s_ref[...] = s0_ref[...]

    q = q_ref[...] * (GLA_DKH ** -0.5)
    k = k_ref[...]
    v = v_ref[...]
    vb = v.astype(BF16)
    row = lax.broadcasted_iota(jnp.int32, (C, C), 0)
    col = lax.broadcasted_iota(jnp.int32, (C, C), 1)
    tri = (row >= col).astype(BF16)
    b = _dot_exact_lhs(tri, lg_ref[...])
    state = s_ref[...]
    inter = jnp.dot((q * jnp.exp(b)).astype(BF16), state.astype(BF16), preferred_element_type=F32)

    t_idx = lax.broadcasted_iota(jnp.int32, (R, 1), 0)
    for i in range(C // R):
        lo = i * R
        bi = b[lo:lo + R]
        qi = q[lo:lo + R]
        ki = k[lo:lo + R]
        vi = v[lo:lo + R]
        oi = inter[lo:lo + R]
        if i > 0:
            b_ref_row = b[lo - 1:lo]
            qe = (qi * jnp.exp(bi - b_ref_row)).astype(BF16)
            ke = (k[:lo] * jnp.exp(b_ref_row - b[:lo])).astype(BF16)
            a = lax.dot_general(qe, ke, (((1,), (1,)), ((), ())), preferred_element_type=F32)
            oi = oi + jnp.dot(a.astype(BF16), vb[:lo], preferred_element_type=F32)
        for s in range(R):
            rel = jnp.where(t_idx >= s, bi - bi[s:s + 1], NEG_INF)
            w = jnp.sum(qi * ki[s:s + 1] * jnp.exp(rel), axis=-1, keepdims=True)
            oi = oi + w * vi[s:s + 1]
        o_ref[lo:lo + R, :] = oi

    b_t = b.T
    b_last = b_t[:, C - 1:C]
    ke_t = (k.T * jnp.exp(b_last - b_t)).astype(BF16)
    s_ref[...] = jnp.exp(b_last) * state + jnp.dot(ke_t, vb, preferred_element_type=F32)


def gla_scan(z, lg, s0):
    bsz, t, _ = z.shape
    C = GLA_CHUNK
    kq = GLA_DK // GLA_DKH
    return pl.pallas_call(
        _gla_scan_kernel,
        grid=(bsz, GLA_HEADS, t // C),
        in_specs=[
            pl.BlockSpec((None, C, GLA_DKH), lambda b, h, c: (b, c, h)),
            pl.BlockSpec((None, C, GLA_DKH), lambda b, h, c: (b, c, kq + h)),
            pl.BlockSpec((None, C, GLA_DVH), lambda b, h, c: (b, c, (2 * GLA_DK) // GLA_DVH + h)),
            pl.BlockSpec((None, C, GLA_DKH), lambda b, h, c: (b, c, h)),
            pl.BlockSpec((None, None, GLA_DKH, GLA_DVH), lambda b, h, c: (b, h, 0, 0)),
        ],
        out_specs=[
            pl.BlockSpec((None, C, GLA_DVH), lambda b, h, c: (b, c, h)),
            pl.BlockSpec((None, None, GLA_DKH, GLA_DVH), lambda b, h, c: (b, h, 0, 0)),
        ],
        out_shape=[jax.ShapeDtypeStruct((bsz, t, GLA_DV), F32),
                   jax.ShapeDtypeStruct((bsz, GLA_HEADS, GLA_DKH, GLA_DVH), F32)],
        compiler_params=_params("arbitrary", "arbitrary", "arbitrary"),
        name="gla_scan",
    )(z, z, z, lg, s0)


def _gla_out_kernel(o_ref, g_ref, ng_ref, w_ref, x_ref, ga_ref, y_ref, p_scr):
    @pl.when(pl.program_id(1) == 0)
    def _():
        gate = _silu(g_ref[...])
        for h in range(GLA_HEADS):
            sl = slice(h * GLA_DVH, (h + 1) * GLA_DVH)
            o = o_ref[:, sl]
            ms = jnp.mean(o * o, axis=-1, keepdims=True)
            p_scr[:, sl] = ((o * lax.rsqrt(ms + RMS_EPS)) * ng_ref[...] * gate[:, sl]).astype(BF16)

    y_ref[...] = x_ref[...] + ga_ref[...] * jnp.dot(p_scr[...], w_ref[...], preferred_element_type=F32)


def _fox_out_kernel(o_ref, g_ref, w_ref, x_ref, ga_ref, y_ref, p_scr):
    @pl.when(pl.program_id(1) == 0)
    def _():
        p_scr[...] = (o_ref[...] * jax.nn.sigmoid(g_ref[...])).astype(BF16)

    y_ref[...] = x_ref[...] + ga_ref[...] * jnp.dot(p_scr[...], w_ref[...], preferred_element_type=F32)


def out_proj(o, z, g_block, norm_gain, w_out, x, mods, tm, rows_per_batch, tn=512):
    m = x.shape[0]
    row_spec = pl.BlockSpec((tm, D_MODEL), lambda i, j: (i, 0))
    in_specs = [row_spec, pl.BlockSpec((tm, D_MODEL), lambda i, j: (i, g_block))]
    args = [o, z]
    if norm_gain is not None:
        in_specs.append(pl.BlockSpec((1, GLA_DVH), lambda i, j: (0, 0)))
        args.append(norm_gain.reshape(1, GLA_DVH))
        body = _gla_out_kernel
    else:
        body = _fox_out_kernel
    in_specs += [pl.BlockSpec((D_MODEL, tn), lambda i, j: (0, j)),
                 pl.BlockSpec((tm, tn), lambda i, j: (i, j))]
    args += [w_out, x]
    if mods.ndim == 3:
        ga_spec = pl.BlockSpec((None, 1, tn),
                               lambda i, j: ((i * tm) // rows_per_batch, 0, 2 * (D_MODEL // tn) + j))
    else:
        ga_spec = pl.BlockSpec((tm, tn), lambda i, j: (i, 2 * (D_MODEL // tn) + j))
    in_specs.append(ga_spec)
    args.append(mods)
    return pl.pallas_call(
        body,
        grid=(m // tm, D_MODEL // tn),
        in_specs=in_specs,
        out_specs=pl.BlockSpec((tm, tn), lambda i, j: (i, j)),
        out_shape=jax.ShapeDtypeStruct((m, D_MODEL), F32),
        scratch_shapes=[pltpu.VMEM((tm, D_MODEL), BF16)],
        compiler_params=_params("arbitrary", "arbitrary"),
        name="out_proj",
    )(*args)


def _fox_in_kernel(x_ref, g_ref, sh_ref, sc_ref, w_ref, wf_ref, bf_ref, qn_ref, kn_ref, z_ref, lf_ref, h_scr,
                   *, tn):
    j = pl.program_id(1)

    @pl.when(j == 0)
    def _():
        h = _modulate(x_ref[...], g_ref[...], sh_ref[...], sc_ref[...]).astype(BF16)
        h_scr[...] = h
        f = jnp.dot(h, wf_ref[...], preferred_element_type=F32)
        lf_ref[...] = _log_sigmoid(f + bf_ref[...])

    acc = jnp.dot(h_scr[...], w_ref[...], preferred_element_type=F32)
    qk_tiles = FOX_DIM // tn

    @pl.when(j < 2 * qk_tiles)
    def _():
        gain = jnp.where(j < qk_tiles, qn_ref[...], kn_ref[...])
        for c in range(tn // FOX_HD):
            blk = acc[:, c * FOX_HD:(c + 1) * FOX_HD]
            ms = jnp.mean(blk * blk, axis=-1, keepdims=True)
            z_ref[:, c * FOX_HD:(c + 1) * FOX_HD] = (blk * lax.rsqrt(ms + RMS_EPS)) * gain

    @pl.when(j >= 2 * qk_tiles)
    def _():
        z_ref[...] = acc


def fox_in_proj(x, gain, mods, w_in, b_f, q_norm, k_norm, tm, rows_per_batch, tn=512):
    m = x.shape[0]
    n_main = 4 * FOX_DIM
    w_f = w_in[:, n_main:]
    sh_spec, sc_spec = _mod_specs(mods, (0, 1), tm, rows_per_batch)
    return pl.pallas_call(
        functools.partial(_fox_in_kernel, tn=tn),
        grid=(m // tm, n_main // tn),
        in_specs=[
            pl.BlockSpec((tm, D_MODEL), lambda i, j: (i, 0)),
            pl.BlockSpec((1, D_MODEL), lambda i, j: (0, 0)),
            sh_spec, sc_spec,
            pl.BlockSpec((D_MODEL, tn), lambda i, j: (0, j)),
            pl.BlockSpec((D_MODEL, FOX_HEADS), lambda i, j: (0, 0)),
            pl.BlockSpec((1, FOX_HEADS), lambda i, j: (0, 0)),
            pl.BlockSpec((1, FOX_HD), lambda i, j: (0, 0)),
            pl.BlockSpec((1, FOX_HD), lambda i, j: (0, 0)),
        ],
        out_specs=[
            pl.BlockSpec((tm, tn), lambda i, j: (i, j)),
            pl.BlockSpec((tm, FOX_HEADS), lambda i, j: (i, 0)),
        ],
        out_shape=[jax.ShapeDtypeStruct((m, n_main), F32), jax.ShapeDtypeStruct((m, FOX_HEADS), F32)],
        scratch_shapes=[pltpu.VMEM((tm, D_MODEL), BF16)],
        compiler_params=_params("arbitrary", "arbitrary"),
        name="fox_in_proj",
    )(x, gain.reshape(1, D_MODEL), mods, mods, w_in, w_f, b_f.reshape(1, FOX_HEADS),
      q_norm.reshape(1, FOX_HD), k_norm.reshape(1, FOX_HD))


def _cumsum_kernel(x_ref, o_ref):
    t = x_ref.shape[0]
    blk = LANES
    row = lax.broadcasted_iota(jnp.int32, (blk, blk), 0)
    col = lax.broadcasted_iota(jnp.int32, (blk, blk), 1)
    tri = (row >= col).astype(BF16)
    carry = jnp.zeros((1, x_ref.shape[1]), F32)
    for i in range(t // blk):
        c = _dot_exact_lhs(tri, x_ref[i * blk:(i + 1) * blk, :]) + carry
        o_ref[i * blk:(i + 1) * blk, :] = c
        carry = c[blk - 1:blk]


def cumsum_time(x):
    bsz, t, h = x.shape
    return pl.pallas_call(
        _cumsum_kernel,
        grid=(bsz,),
        in_specs=[pl.BlockSpec((None, t, h), lambda b: (b, 0, 0))],
        out_specs=pl.BlockSpec((None, t, h), lambda b: (b, 0, 0)),
        out_shape=jax.ShapeDtypeStruct((bsz, t, h), F32),
        compiler_params=_params("arbitrary"),
        name="cumsum_time",
    )(x)


def _fox_attn_kernel(q_ref, k_ref, v_ref, cc_ref, cr_ref, o_ref):
    blk = ATTN_BLOCK
    t = q_ref.shape[0]
    nt_dims = (((1,), (1,)), ((), ()))
    kb = k_ref[...].astype(BF16)
    vb = v_ref[...].astype(BF16)
    row = lax.broadcasted_iota(jnp.int32, (blk, blk), 0)
    col = lax.broadcasted_iota(jnp.int32, (blk, blk), 1)
    for i in range(t // blk):
        lo = i * blk
        q = (q_ref[lo:lo + blk, :] * (FOX_HD ** -0.5)).astype(BF16)
        cq = cc_ref[lo:lo + blk, :]
        s_d = lax.dot_general(q, kb[lo:lo + blk], nt_dims, preferred_element_type=F32)
        s_d = jnp.where(col <= row, s_d + cq - cr_ref[:, lo:lo + blk], NEG_INF)
        m = jnp.max(s_d, axis=-1, keepdims=True)
        if i > 0:
            s_p = lax.dot_general(q, kb[:lo], nt_dims, preferred_element_type=F32) + cq - cr_ref[:, :lo]
            m = jnp.maximum(m, jnp.max(s_p, axis=-1, keepdims=True))
        p_d = jnp.exp(s_d - m)
        l = jnp.sum(p_d, axis=-1, keepdims=True)
        acc = jnp.dot(p_d.astype(BF16), vb[lo:lo + blk], preferred_element_type=F32)
        if i > 0:
            p_p = jnp.exp(s_p - m)
            l = l + jnp.sum(p_p, axis=-1, keepdims=True)
            acc = acc + jnp.dot(p_p.astype(BF16), vb[:lo], preferred_element_type=F32)
        o_ref[lo:lo + blk, :] = acc / l


def fox_prompt_attn(z, cum):
    bsz, t, _ = z.shape
    cum_h = jnp.transpose(cum, (0, 2, 1))
    cum_col = cum_h.reshape(bsz, FOX_HEADS, t, 1)
    cum_row = cum_h.reshape(bsz, FOX_HEADS, 1, t)
    return pl.pallas_call(
        _fox_attn_kernel,
        grid=(bsz, FOX_HEADS),
        in_specs=[
            pl.BlockSpec((None, t, FOX_HD), lambda b, h: (b, 0, h)),
            pl.BlockSpec((None, t, FOX_HD), lambda b, h: (b, 0, FOX_HEADS + h)),
            pl.BlockSpec((None, t, FOX_HD), lambda b, h: (b, 0, 2 * FOX_HEADS + h)),
            pl.BlockSpec((None, None, t, 1), lambda b, h: (b, h, 0, 0)),
            pl.BlockSpec((None, None, 1, t), lambda b, h: (b, h, 0, 0)),
        ],
        out_specs=pl.BlockSpec((None, t, FOX_HD), lambda b, h: (b, 0, h)),
        out_shape=jax.ShapeDtypeStruct((bsz, t, FOX_DIM), F32),
        compiler_params=_params("arbitrary", "arbitrary"),
        name="fox_prompt_attn",
    )(z, z, z, cum_col, cum_row)


def _fox_suffix_kernel(pt_ref, *rest):
    G = SUFFIX_PAGES_PER_STEP
    lf_refs = rest[:G]
    o_ref, carry_scr = rest[G:]
    page = lf_refs[0].shape[0]

    @pl.when(pl.program_id(1) == 0)
    def _():
        carry_scr[...] = jnp.zeros(carry_scr.shape, F32)

    row = lax.broadcasted_iota(jnp.int32, (page, page), 0)
    col = lax.broadcasted_iota(jnp.int32, (page, page), 1)
    upper = (col > row).astype(BF16)
    carry = carry_scr[...]
    for g in range(G):
        lf = lf_refs[g][...]
        o_ref[G - 1 - g] = _dot_exact_lhs(upper, lf) + carry
        carry = carry + jnp.sum(lf, axis=0, keepdims=True)
    carry_scr[...] = carry


def fox_suffix(cache_logf, layer, page_table):
    db, n_pages = page_table.shape
    page = cache_logf.shape[2]
    G = SUFFIX_PAGES_PER_STEP
    steps = n_pages // G
    lf_specs = [pl.BlockSpec((None, None, page, FOX_HEADS),
                             lambda b, t, pt, g=g: (layer, pt[b, n_pages - 1 - (t * G + g)], 0, 0))
                for g in range(G)]
    grid_spec = pltpu.PrefetchScalarGridSpec(
        num_scalar_prefetch=1,
        grid=(db, steps),
        in_specs=lf_specs,
        out_specs=pl.BlockSpec((None, G, page, FOX_HEADS), lambda b, t, pt: (b, steps - 1 - t, 0, 0)),
        scratch_shapes=[pltpu.VMEM((1, FOX_HEADS), F32)],
    )
    return pl.pallas_call(
        _fox_suffix_kernel,
        grid_spec=grid_spec,
        out_shape=jax.ShapeDtypeStruct((db, n_pages, page, FOX_HEADS), F32),
        compiler_params=_params("arbitrary", "arbitrary"),
        name="fox_suffix",
    )(page_table, *([cache_logf] * G))


def _fox_paged_kernel(pt_ref, q_ref, ct_ref, kn_ref, vn_ref, bn_ref, suf_ref, *rest):
    G = PAGES_PER_STEP
    k_refs = rest[:G]
    v_refs = rest[G:2 * G]
    o_ref = rest[2 * G]
    m_scr, l_scr, acc_scr = rest[2 * G + 1:]
    t = pl.program_id(1)
    page = k_refs[0].shape[0]
    rows = q_ref.shape[0]
    nt_dims = (((1,), (1,)), ((), ()))

    @pl.when(t == 0)
    def _():
        m_scr[...] = jnp.full(m_scr.shape, NEG_INF, F32)
        l_scr[...] = jnp.zeros(l_scr.shape, F32)
        acc_scr[...] = jnp.zeros(acc_scr.shape, F32)

    q = q_ref[...].astype(BF16)
    r_id = lax.broadcasted_iota(jnp.int32, (rows, LANES), 0)
    c_id = lax.broadcasted_iota(jnp.int32, (rows, LANES), 1)
    head_bias = jnp.where(r_id % FOX_HEADS == c_id % FOX_HEADS, 0.0, NEG_INF)
    head_bias = jnp.concatenate([head_bias] * (page * FOX_HEADS // LANES), axis=1) + ct_ref[...]

    def online(carry, s_list, v_list):
        m, l, acc = carry
        m_new = m
        for s in s_list:
            m_new = jnp.maximum(m_new, jnp.max(s, axis=-1, keepdims=True))
        alpha = jnp.exp(m - m_new)
        l = alpha * l
        acc = alpha * acc
        for s, v2 in zip(s_list, v_list):
            p = jnp.exp(s - m_new)
            l = l + jnp.sum(p, axis=-1, keepdims=True)
            acc = acc + jnp.dot(p.astype(BF16), v2, preferred_element_type=F32)
        return m_new, l, acc

    s_list, v_list = [], []
    for g in range(G):
        k2 = k_refs[g][...].reshape(page * FOX_HEADS, FOX_HD).astype(BF16)
        v_list.append(v_refs[g][...].reshape(page * FOX_HEADS, FOX_HD).astype(BF16))
        s_list.append(lax.dot_general(q, k2, nt_dims, preferred_element_type=F32) + (head_bias + suf_ref[g]))
    carry = online((m_scr[...], l_scr[...], acc_scr[...]), s_list, v_list)
    m_scr[...], l_scr[...], acc_scr[...] = carry

    @pl.when(t == pl.num_programs(1) - 1)
    def _():
        s = lax.dot_general(q, kn_ref[...].astype(BF16), nt_dims, preferred_element_type=F32) + bn_ref[...]
        _, l, acc = online(carry, [s], [vn_ref[...].astype(BF16)])
        o_ref[...] = acc / l


def fox_paged_attn(q, k_new, v_new, logf_new, cache_k, cache_v, cache_logf, layer, page_table):
    db, s_len, _, _ = q.shape
    n_pages = page_table.shape[1]
    page = cache_k.shape[2]
    G = PAGES_PER_STEP
    rows = s_len * FOX_HEADS
    new_cols = LANES
    assert rows % 8 == 0 and rows <= new_cols and (page * FOX_HEADS) % LANES == 0

    suf = fox_suffix(cache_logf, layer, page_table).reshape(db, n_pages, 1, page * FOX_HEADS)

    q2 = (q * (FOX_HD ** -0.5)).reshape(db, rows, FOX_HD)
    cum = jnp.cumsum(logf_new, axis=1)
    ct = cum.reshape(db, rows, 1)
    same_head = jnp.arange(FOX_HEADS)[:, None] == jnp.arange(FOX_HEADS)[None, :]
    causal = jnp.arange(s_len)[None, :] <= jnp.arange(s_len)[:, None]
    ok = causal[:, None, :, None] & same_head[None, :, None, :]
    bn = jnp.where(ok[None], cum[:, :, :, None, None] - cum[:, None, None, :, :], NEG_INF)
    bn = jnp.pad(bn.reshape(db, rows, rows), ((0, 0), (0, 0), (0, new_cols - rows)), constant_values=NEG_INF)
    kn = jnp.pad(k_new.reshape(db, rows, FOX_HD), ((0, 0), (0, new_cols - rows), (0, 0)))
    vn = jnp.pad(v_new.reshape(db, rows, FOX_HD), ((0, 0), (0, new_cols - rows), (0, 0)))

    kv_specs = [pl.BlockSpec((None, None, page, FOX_HEADS, FOX_HD),
                             lambda b, t, pt, g=g: (layer, pt[b, t * G + g], 0, 0, 0)) for g in range(G)]
    grid_spec = pltpu.PrefetchScalarGridSpec(
        num_scalar_prefetch=1,
        grid=(db, n_pages // G),
        in_specs=[
            pl.BlockSpec((None, rows, FOX_HD), lambda b, t, pt: (b, 0, 0)),
            pl.BlockSpec((None, rows, 1), lambda b, t, pt: (b, 0, 0)),
            pl.BlockSpec((None, new_cols, FOX_HD), lambda b, t, pt: (b, 0, 0)),
            pl.BlockSpec((None, new_cols, FOX_HD), lambda b, t, pt: (b, 0, 0)),
            pl.BlockSpec((None, rows, new_cols), lambda b, t, pt: (b, 0, 0)),
            pl.BlockSpec((None, G, 1, page * FOX_HEADS), lambda b, t, pt: (b, t, 0, 0)),
        ] + kv_specs + kv_specs,
        out_specs=pl.BlockSpec((None, rows, FOX_HD), lambda b, t, pt: (b, 0, 0)),
        scratch_shapes=[pltpu.VMEM((rows, 1), F32), pltpu.VMEM((rows, 1), F32),
                        pltpu.VMEM((rows, FOX_HD), F32)],
    )
    o = pl.pallas_call(
        _fox_paged_kernel,
        grid_spec=grid_spec,
        out_shape=jax.ShapeDtypeStruct((db, rows, FOX_HD), F32),
        compiler_params=_params("arbitrary", "arbitrary"),
        name="fox_paged_attn",
    )(page_table, q2, ct, kn, vn, bn, suf, *([cache_k] * G), *([cache_v] * G))
    return o.reshape(db, s_len, FOX_HEADS, FOX_HD)


def _moe_pre_kernel(x_ref, g_ref, sh_ref, sc_ref, wr_ref, br_ref, h_ref, lo_ref):
    h = _modulate(x_ref[...], g_ref[...], sh_ref[...], sc_ref[...]).astype(BF16)
    h_ref[...] = h
    lo_ref[...] = jnp.dot(h, wr_ref[...], preferred_element_type=F32) + br_ref[...]


def moe_pre(x, gain, mods, w_router, b_router, tm, rows_per_batch):
    m = x.shape[0]
    sh_spec, sc_spec = _mod_specs(mods, (3, 4), tm, rows_per_batch)
    return pl.pallas_call(
        _moe_pre_kernel,
        grid=(m // tm, 1),
        in_specs=[
            pl.BlockSpec((tm, D_MODEL), lambda i, j: (i, 0)),
            pl.BlockSpec((1, D_MODEL), lambda i, j: (0, 0)),
            sh_spec, sc_spec,
            pl.BlockSpec((D_MODEL, ROUTER_COLS), lambda i, j: (0, 0)),
            pl.BlockSpec((1, ROUTER_COLS), lambda i, j: (0, 0)),
        ],
        out_specs=[
            pl.BlockSpec((tm, D_MODEL), lambda i, j: (i, 0)),
            pl.BlockSpec((tm, ROUTER_COLS), lambda i, j: (i, 0)),
        ],
        out_shape=[jax.ShapeDtypeStruct((m, D_MODEL), BF16), jax.ShapeDtypeStruct((m, ROUTER_COLS), F32)],
        compiler_params=_params("arbitrary", "arbitrary"),
        name="moe_pre",
    )(x, gain.reshape(1, D_MODEL), mods, mods, w_router, b_router)


def _moe_expert_kernel(be_ref, nu_ref, x_ref, wg_ref, wu_ref, wd_ref, y_ref, wg_s, wu_s, wd_s):
    i = pl.program_id(0)
    prev = be_ref[jnp.maximum(i - 1, 0)]

    @pl.when((i == 0) | (be_ref[i] != prev))
    def _():
        wg_s[...] = wg_ref[...].astype(BF16)
        wu_s[...] = wu_ref[...].astype(BF16)
        wd_s[...] = wd_ref[...].astype(BF16)

    @pl.when(i < nu_ref[0])
    def _():
        x = x_ref[...]
        a = jnp.dot(x, wg_s[...], preferred_element_type=F32)
        u = jnp.dot(x, wu_s[...], preferred_element_type=F32)
        y_ref[...] = jnp.dot((_silu(a) * u).astype(BF16), wd_s[...], preferred_element_type=F32)

    @pl.when(i >= nu_ref[0])
    def _():
        y_ref[...] = jnp.zeros(y_ref.shape, F32)


def moe_experts(xs, block_e, n_used, w_gate, w_up, w_down, layer):
    cap = xs.shape[0]
    tb = MOE_BLOCK
    grid_spec = pltpu.PrefetchScalarGridSpec(
        num_scalar_prefetch=2,
        grid=(cap // tb,),
        in_specs=[
            pl.BlockSpec((tb, D_MODEL), lambda i, be, nu: (i, 0)),
            pl.BlockSpec((None, None, D_MODEL, MOE_DFF), lambda i, be, nu: (layer, be[i], 0, 0)),
            pl.BlockSpec((None, None, D_MODEL, MOE_DFF), lambda i, be, nu: (layer, be[i], 0, 0)),
            pl.BlockSpec((None, None, MOE_DFF, D_MODEL), lambda i, be, nu: (layer, be[i], 0, 0)),
        ],
        out_specs=pl.BlockSpec((tb, D_MODEL), lambda i, be, nu: (i, 0)),
        scratch_shapes=[pltpu.VMEM((D_MODEL, MOE_DFF), BF16), pltpu.VMEM((D_MODEL, MOE_DFF), BF16),
                        pltpu.VMEM((MOE_DFF, D_MODEL), BF16)],
    )
    return pl.pallas_call(
        _moe_expert_kernel,
        grid_spec=grid_spec,
        out_shape=jax.ShapeDtypeStruct((cap, D_MODEL), F32),
        compiler_params=_params("arbitrary"),
        name="moe_experts",
    )(block_e, n_used, xs, w_gate, w_up, w_down)


def _route_kernel(lo_ref, eid_ref, gw_ref, rank_ref, cnt_ref, carry_scr, *, n_valid):
    tm = lo_ref.shape[0]
    i = pl.program_id(0)

    @pl.when(i == 0)
    def _():
        carry_scr[...] = jnp.zeros(carry_scr.shape, F32)

    lo = lo_ref[...]
    lane_i = lax.broadcasted_iota(jnp.int32, lo.shape, 1)
    lane = lane_i.astype(F32)
    row = lax.broadcasted_iota(jnp.int32, lo.shape, 0) + i * tm

    def masked_softmax(mask):
        x = jnp.where(mask, lo, NEG_INF)
        e = jnp.where(mask, jnp.exp(x - jnp.max(x, axis=-1, keepdims=True)), 0.0)
        return e / jnp.sum(e, axis=-1, keepdims=True)

    def first_max(p, mask):
        pm = jnp.where(mask, p, -1.0)
        top = jnp.max(pm, axis=-1, keepdims=True)
        idx = jnp.min(jnp.where(pm == top, lane, float(LANES)), axis=-1, keepdims=True)
        return top, idx

    gmask = lane_i < MOE_GROUPS
    p_g, g_idx = first_max(masked_softmax(gmask), gmask)
    e_lane = lane_i - MOE_GROUPS
    emask = (e_lane >= 0) & (e_lane < MOE_EXPERTS) & ((e_lane // MOE_PER_GROUP).astype(F32) == g_idx)
    pe = masked_softmax(emask)
    w0, i0 = first_max(pe, emask)
    w1, i1 = first_max(pe, emask & (lane != i0))
    tw = w0 + w1
    two = lax.broadcasted_iota(jnp.int32, (tm, MOE_TOPK), 1)
    gw_ref[...] = jnp.where(two == 0, p_g * (w0 / tw), p_g * (w1 / tw))
    eid_ref[...] = jnp.where(two == 0, i0, i1).astype(jnp.int32) - MOE_GROUPS

    cnt = jnp.where((row < n_valid) & ((lane == i0) | (lane == i1)), 1.0, 0.0)
    r_id = lax.broadcasted_iota(jnp.int32, (tm, tm), 0)
    c_id = lax.broadcasted_iota(jnp.int32, (tm, tm), 1)
    earlier = (c_id < r_id).astype(BF16)
    before = jnp.dot(earlier, cnt.astype(BF16), preferred_element_type=F32) + carry_scr[...]
    rank0 = jnp.sum(jnp.where(lane == i0, before, 0.0), axis=-1, keepdims=True)
    rank1 = jnp.sum(jnp.where(lane == i1, before, 0.0), axis=-1, keepdims=True)
    rank_ref[...] = jnp.where(two == 0, rank0, rank1).astype(jnp.int32)
    total = carry_scr[...] + jnp.sum(cnt, axis=0, keepdims=True)
    carry_scr[...] = total
    cnt_ref[...] = total


def moe_route(logits, n_valid):
    n_pad = logits.shape[0]
    tm = ROUTE_BLOCK
    pair = lambda dt: jax.ShapeDtypeStruct((n_pad, MOE_TOPK), dt)
    pair_spec = pl.BlockSpec((tm, MOE_TOPK), lambda i: (i, 0))
    return pl.pallas_call(
        functools.partial(_route_kernel, n_valid=n_valid),
        grid=(n_pad // tm,),
        in_specs=[pl.BlockSpec((tm, ROUTER_COLS), lambda i: (i, 0))],
        out_specs=[pair_spec, pair_spec, pair_spec, pl.BlockSpec((1, ROUTER_COLS), lambda i: (0, 0))],
        out_shape=[pair(jnp.int32), pair(F32), pair(jnp.int32), jax.ShapeDtypeStruct((1, ROUTER_COLS), F32)],
        scratch_shapes=[pltpu.VMEM((1, ROUTER_COLS), F32)],
        compiler_params=_params("arbitrary"),
        name="moe_route",
    )(logits)


def _dispatch(e_ids, rank, lane_counts):
    n = e_ids.shape[0]
    tb = MOE_BLOCK
    counts = lane_counts[0, MOE_GROUPS:MOE_GROUPS + MOE_EXPERTS].astype(jnp.int32)
    padded = ((counts + tb - 1) // tb) * tb
    pend = jnp.cumsum(padded)
    pstart = pend - padded
    dest = pstart[e_ids] + rank
    n_blocks = (n * MOE_TOPK + MOE_EXPERTS * (tb - 1) + tb - 1) // tb
    tok = jnp.repeat(jnp.arange(n, dtype=jnp.int32), MOE_TOPK)
    buf_tok = jnp.zeros((n_blocks * tb,), jnp.int32).at[dest.reshape(-1)].set(tok)
    block_e = jnp.minimum(jnp.searchsorted(pend, jnp.arange(n_blocks, dtype=jnp.int32) * tb, side='right'),
                          MOE_EXPERTS - 1).astype(jnp.int32)
    n_used = (pend[-1] // tb).astype(jnp.int32).reshape(1)
    return dest, buf_tok, block_e, n_used


def _combine_kernel(x_ref, ga_ref, y0_ref, y1_ref, w_ref, o_ref):
    w = w_ref[...]
    moe = y0_ref[...] * w[:, 0:1] + y1_ref[...] * w[:, 1:2]
    o_ref[...] = x_ref[...] + ga_ref[...] * moe


def moe_combine(x, mods, y0, y1, gate_w, tm, rows_per_batch):
    m = x.shape[0]
    (ga_spec,) = _mod_specs(mods, (5,), tm, rows_per_batch)
    row_spec = pl.BlockSpec((tm, D_MODEL), lambda i, j: (i, 0))
    return pl.pallas_call(
        _combine_kernel,
        grid=(m // tm, 1),
        in_specs=[row_spec, ga_spec, row_spec, row_spec, pl.BlockSpec((tm, MOE_TOPK), lambda i, j: (i, 0))],
        out_specs=row_spec,
        out_shape=jax.ShapeDtypeStruct((m, D_MODEL), F32),
        compiler_params=_params("arbitrary", "arbitrary"),
        name="moe_combine",
    )(x, mods, y0, y1, gate_w)


def _final_norm_kernel(x_ref, g_ref, o_ref):
    x = x_ref[...]
    ms = jnp.mean(x * x, axis=-1, keepdims=True)
    o_ref[...] = (x * lax.rsqrt(ms + RMS_EPS)) * g_ref[...]


def final_norm(x, gain, tm):
    m = x.shape[0]
    row_spec = pl.BlockSpec((tm, D_MODEL), lambda i: (i, 0))
    return pl.pallas_call(
        _final_norm_kernel,
        grid=(m // tm,),
        in_specs=[row_spec, pl.BlockSpec((1, D_MODEL), lambda i: (0, 0))],
        out_specs=row_spec,
        out_shape=jax.ShapeDtypeStruct((m, D_MODEL), F32),
        compiler_params=_params("arbitrary"),
        name="final_norm",
    )(x, gain.reshape(1, D_MODEL))


def kernel(x_prompt, x_sample, c_prompt, c_sample, state_gla, cache_k, cache_v, cache_logf, page_table, norm_mix, norm_ffn, norm_final, w_ada, b_ada, gla_w_in, gla_w_gate_up, gla_b_gate, gla_norm, gla_w_out, fox_w_in, fox_b_f, fox_q_norm, fox_k_norm, fox_w_out, moe_w_group, moe_b_group, moe_w_expert, moe_b_expert, moe_w_gate, moe_w_up, moe_w_down):
    bsz, seq, d = x_prompt.shape
    db, ds, _ = x_sample.shape
    depth = w_ada.shape[0]
    mp, msz = bsz * seq, db * ds
    ti_p = min(1024, seq)
    tm_p = min(512, seq)
    te_p = min(256, seq)
    n_tok = mp + msz
    n_route = -(-n_tok // ROUTE_BLOCK) * ROUTE_BLOCK
    assert d == D_MODEL and seq % ti_p == 0 and seq % GLA_CHUNK == 0 and seq % ATTN_BLOCK == 0
    assert ds <= GLA_CHUNK and page_table.shape[1] % PAGES_PER_STEP == 0
    assert page_table.shape[1] % SUFFIX_PAGES_PER_STEP == 0

    gla_w_in_b, gla_w_up_b, gla_w_out_b = (w.astype(BF16) for w in (gla_w_in, gla_w_gate_up, gla_w_out))
    fox_w_in_b, fox_w_out_b = fox_w_in.astype(BF16), fox_w_out.astype(BF16)

    c_rows = jnp.concatenate([c_prompt, c_sample], axis=0)
    c_rows = jnp.pad(c_rows, ((0, (-c_rows.shape[0]) % 8), (0, 0)))
    ada = ada_all(c_rows, w_ada, b_ada)

    xp = x_prompt.reshape(mp, d)
    xs = x_sample.reshape(msz, d)
    gla_p, gla_s, kp_l, vp_l, lfp_l, ks_l, vs_l, lfs_l = [], [], [], [], [], [], [], []
    for i in range(depth):
        j = i // 2
        mods_p = ada[i, :bsz].reshape(bsz, 1, 6 * d)
        mods_s = jnp.repeat(ada[i, bsz:bsz + db], ds, axis=0)
        if i % 2 == 0:
            zp, lgp = gla_in_proj(xp, norm_mix[i], mods_p, gla_w_in_b[j], gla_w_up_b[j], gla_b_gate[j],
                                  ti_p, seq)
            s0 = jnp.zeros((bsz, GLA_HEADS, GLA_DKH, GLA_DVH), F32)
            op, s_fin = gla_scan(zp.reshape(bsz, seq, GLA_MAIN), lgp.reshape(bsz, seq, GLA_DK), s0)
            xp = out_proj(op.reshape(mp, d), zp, 2, gla_norm[j], gla_w_out_b[j], xp, mods_p, tm_p, seq)
            gla_p.append(s_fin)

            zs, lgs = gla_in_proj(xs, norm_mix[i], mods_s, gla_w_in_b[j], gla_w_up_b[j], gla_b_gate[j],
                                  msz, msz)
            pad = ((0, 0), (0, GLA_CHUNK - ds), (0, 0))
            zs_pad = jnp.pad(zs.reshape(db, ds, GLA_MAIN), pad)
            lgs_pad = jnp.pad(lgs.reshape(db, ds, GLA_DK), pad)
            os_pad, s_new = gla_scan(zs_pad, lgs_pad, state_gla[j])
            xs = out_proj(os_pad[:, :ds].reshape(msz, d), zs, 2, gla_norm[j], gla_w_out_b[j], xs, mods_s,
                          msz, msz)
            gla_s.append(s_new)
        else:
            zp, lfp = fox_in_proj(xp, norm_mix[i], mods_p, fox_w_in_b[j], fox_b_f[j], fox_q_norm[j],
                                  fox_k_norm[j], ti_p, seq)
            z3 = zp.reshape(bsz, seq, 4 * FOX_DIM)
            lf3 = lfp.reshape(bsz, seq, FOX_HEADS)
            op = fox_prompt_attn(z3, cumsum_time(lf3))
            xp = out_proj(op.reshape(mp, d), zp, 3, None, fox_w_out_b[j], xp, mods_p, tm_p, seq)
            kp_l.append(z3[:, :, FOX_DIM:2 * FOX_DIM].reshape(bsz, seq, FOX_HEADS, FOX_HD))
            vp_l.append(z3[:, :, 2 * FOX_DIM:3 * FOX_DIM].reshape(bsz, seq, FOX_HEADS, FOX_HD))
            lfp_l.append(lf3)

            zs, lfs = fox_in_proj(xs, norm_mix[i], mods_s, fox_w_in_b[j], fox_b_f[j], fox_q_norm[j],
                                  fox_k_norm[j], msz, msz)
            zs4 = zs.reshape(db, ds, 4, FOX_HEADS, FOX_HD)
            q_s, k_s, v_s = zs4[:, :, 0], zs4[:, :, 1], zs4[:, :, 2]
            lfs3 = lfs.reshape(db, ds, FOX_HEADS)
            os_ = fox_paged_attn(q_s, k_s, v_s, lfs3, cache_k, cache_v, cache_logf, j, page_table)
            xs = out_proj(os_.reshape(msz, d), zs, 3, None, fox_w_out_b[j], xs, mods_s, msz, msz)
            ks_l.append(k_s)
            vs_l.append(v_s)
            lfs_l.append(lfs3)

        w_router = jnp.pad(jnp.concatenate([moe_w_group[i], moe_w_expert[i]], axis=1),
                           ((0, 0), (0, ROUTER_COLS - MOE_GROUPS - MOE_EXPERTS))).astype(BF16)
        b_router = jnp.pad(jnp.concatenate([moe_b_group[i], moe_b_expert[i]]),
                           (0, ROUTER_COLS - MOE_GROUPS - MOE_EXPERTS)).reshape(1, ROUTER_COLS)
        hp, lop = moe_pre(xp, norm_ffn[i], mods_p, w_router, b_router, tm_p, seq)
        hs, los = moe_pre(xs, norm_ffn[i], mods_s, w_router, b_router, msz, msz)
        h_all = jnp.concatenate([hp, hs], axis=0)
        logits = jnp.concatenate([lop, los, jnp.zeros((n_route - n_tok, ROUTER_COLS), F32)], axis=0)
        e_ids, gate_w, rank, lane_counts = moe_route(logits, n_tok)
        slot, buf_tok, block_e, n_used = _dispatch(e_ids[:n_tok], rank[:n_tok], lane_counts)
        y = moe_experts(h_all[buf_tok], block_e, n_used, moe_w_gate, moe_w_up, moe_w_down, i)
        y0, y1 = y[slot[:, 0]], y[slot[:, 1]]
        xp = moe_combine(xp, mods_p, y0[:mp], y1[:mp], gate_w[:mp], te_p, seq)
        xs = moe_combine(xs, mods_s, y0[mp:], y1[mp:], gate_w[mp:n_tok], msz, msz)

    y_prompt = final_norm(xp, norm_final, te_p).reshape(bsz, seq, d)
    y_sample = final_norm(xs, norm_final, msz).reshape(db, ds, d)
    return (y_prompt, y_sample,
            jnp.stack(kp_l), jnp.stack(vp_l), jnp.stack(lfp_l), jnp.stack(gla_p),
            jnp.stack(ks_l), jnp.stack(vs_l), jnp.stack(lfs_l), jnp.stack(gla_s))
```

```python
import functools

import jax
import jax.numpy as jnp
from jax import lax
from jax.experimental import pallas as pl
from jax.experimental.pallas import tpu as pltpu

F32 = jnp.float32
BF16 = jnp.bfloat16

D_MODEL = 2048
GLA_HEADS = 4
GLA_DK = D_MODEL // 2
GLA_DV = D_MODEL
GLA_DKH = GLA_DK // GLA_HEADS
GLA_DVH = GLA_DV // GLA_HEADS
GLA_GATE_RANK = 16
GLA_GATE_NORM = 16.0
GLA_MAIN = 2 * GLA_DK + 2 * GLA_DV
FOX_HEADS = 16
FOX_HD = D_MODEL // FOX_HEADS
FOX_DIM = FOX_HEADS * FOX_HD
MOE_GROUPS = 4
MOE_PER_GROUP = 8
MOE_EXPERTS = MOE_GROUPS * MOE_PER_GROUP
MOE_TOPK = 2
MOE_DFF = D_MODEL // 4
RMS_EPS = 1e-6
NEG_INF = -1e30

VMEM_LIMIT_BYTES = 52 * 1024 * 1024
LANES = 128

GLA_CHUNK = 128
GLA_SUB = 16
ATTN_BLOCK = 256
PAGES_PER_STEP = 4
SUFFIX_PAGES_PER_STEP = 16
ROUTE_BLOCK = 256
MOE_BLOCK = 256
ROUTER_COLS = 128


def _params(*sem):
    return pltpu.CompilerParams(dimension_semantics=sem, vmem_limit_bytes=VMEM_LIMIT_BYTES)


def _log_sigmoid(x):
    return jnp.minimum(x, 0.0) - jnp.log1p(jnp.exp(-jnp.abs(x)))


def _silu(x):
    return x * jax.nn.sigmoid(x)


def _split3(a):
    hi = a.astype(BF16)
    r1 = a - hi.astype(F32)
    mid = r1.astype(BF16)
    lo = (r1 - mid.astype(F32)).astype(BF16)
    return hi, mid, lo


def _dot_exact_lhs(sel_bf16, x_f32):
    hi, mid, lo = _split3(x_f32)
    d = functools.partial(jnp.dot, preferred_element_type=F32)
    return d(sel_bf16, hi) + d(sel_bf16, mid) + d(sel_bf16, lo)


def _modulate(x, g, shift, scale):
    ms = jnp.mean(x * x, axis=-1, keepdims=True)
    return (x * lax.rsqrt(ms + RMS_EPS)) * g * (1.0 + scale) + shift


def _ada_kernel(c_ref, w_ref, b_ref, o_ref):
    a = _silu(c_ref[...]).astype(BF16)
    o_ref[...] = jnp.dot(a, w_ref[...].astype(BF16), preferred_element_type=F32) + b_ref[...]


def ada_all(c_rows, w_ada, b_ada, tn=1024):
    depth, d, n = w_ada.shape
    rows = c_rows.shape[0]
    return pl.pallas_call(
        _ada_kernel,
        grid=(depth, n // tn),
        in_specs=[
            pl.BlockSpec((rows, d), lambda l, j: (0, 0)),
            pl.BlockSpec((None, d, tn), lambda l, j: (l, 0, j)),
            pl.BlockSpec((None, 1, tn), lambda l, j: (l, 0, j)),
        ],
        out_specs=pl.BlockSpec((None, rows, tn), lambda l, j: (l, 0, j)),
        out_shape=jax.ShapeDtypeStruct((depth, rows, n), F32),
        compiler_params=_params("arbitrary", "arbitrary"),
        name="ada_all",
    )(c_rows, w_ada, b_ada.reshape(depth, 1, n))


def _mod_specs(mods, chunk_ids, tm, rows_per_batch):
    specs = []
    for c in chunk_ids:
        if mods.ndim == 3:
            specs.append(pl.BlockSpec((None, 1, D_MODEL),
                                      lambda i, j, c=c: ((i * tm) // rows_per_batch, 0, c)))
        else:
            specs.append(pl.BlockSpec((tm, D_MODEL), lambda i, j, c=c: (i, c)))
    return specs


def _gla_in_kernel(x_ref, g_ref, sh_ref, sc_ref, w_ref, wgd_ref, wup_ref, bg_ref, z_ref, lg_ref, h_scr):
    @pl.when(pl.program_id(1) == 0)
    def _():
        h = _modulate(x_ref[...], g_ref[...], sh_ref[...], sc_ref[...]).astype(BF16)
        h_scr[...] = h
        gd = jnp.dot(h, wgd_ref[...], preferred_element_type=F32)
        gate = jnp.dot(gd.astype(BF16), wup_ref[...], preferred_element_type=F32) + bg_ref[...]
        lg_ref[...] = _log_sigmoid(gate) * (1.0 / GLA_GATE_NORM)

    z_ref[...] = jnp.dot(h_scr[...], w_ref[...], preferred_element_type=F32)


def gla_in_proj(x, gain, mods, w_in, w_gate_up, b_gate, tm, rows_per_batch, tn=512):
    m = x.shape[0]
    w_gd = w_in[:, GLA_MAIN:]
    sh_spec, sc_spec = _mod_specs(mods, (0, 1), tm, rows_per_batch)
    return pl.pallas_call(
        _gla_in_kernel,
        grid=(m // tm, GLA_MAIN // tn),
        in_specs=[
            pl.BlockSpec((tm, D_MODEL), lambda i, j: (i, 0)),
            pl.BlockSpec((1, D_MODEL), lambda i, j: (0, 0)),
            sh_spec, sc_spec,
            pl.BlockSpec((D_MODEL, tn), lambda i, j: (0, j)),
            pl.BlockSpec((D_MODEL, GLA_GATE_RANK), lambda i, j: (0, 0)),
            pl.BlockSpec((GLA_GATE_RANK, GLA_DK), lambda i, j: (0, 0)),
            pl.BlockSpec((1, GLA_DK), lambda i, j: (0, 0)),
        ],
        out_specs=[
            pl.BlockSpec((tm, tn), lambda i, j: (i, j)),
            pl.BlockSpec((tm, GLA_DK), lambda i, j: (i, 0)),
        ],
        out_shape=[jax.ShapeDtypeStruct((m, GLA_MAIN), F32), jax.ShapeDtypeStruct((m, GLA_DK), F32)],
        scratch_shapes=[pltpu.VMEM((tm, D_MODEL), BF16)],
        compiler_params=_params("arbitrary", "arbitrary"),
        name="gla_in_proj",
    )(x, gain.reshape(1, D_MODEL), mods, mods, w_in, w_gd, w_gate_up, b_gate.reshape(1, GLA_DK))


def _gla_scan_kernel(q_ref, k_ref, v_ref, lg_ref, s0_ref, o_ref, s_ref):
    C, R = GLA_CHUNK, GLA_SUB

    @pl.when(pl.program_id(2) == 0)
    def _():
        s_ref[...] = s0_ref[...]

    q = q_ref[...] * (GLA_DKH ** -0.5)
    k = k_ref[...]
    v = v_ref[...]
    vb = v.astype(BF16)
    row = lax.broadcasted_iota(jnp.int32, (C, C), 0)
    col = lax.broadcasted_iota(jnp.int32, (C, C), 1)
    tri = (row >= col).astype(BF16)
    b = _dot_exact_lhs(tri, lg_ref[...])
    state = s_ref[...]
    inter = jnp.dot((q * jnp.exp(b)).astype(BF16), state.astype(BF16), preferred_element_type=F32)

    t_idx = lax.broadcasted_iota(jnp.int32, (R, 1), 0)
    for i in range(C // R):
        lo = i * R
        bi = b[lo:lo + R]
        qi = q[lo:lo + R]
        ki = k[lo:lo + R]
        vi = v[lo:lo + R]
        oi = inter[lo:lo + R]
        if i > 0:
            b_ref_row = b[lo - 1:lo]
            qe = (qi * jnp.exp(bi - b_ref_row)).astype(BF16)
            ke = (k[:lo] * jnp.exp(b_ref_row - b[:lo])).astype(BF16)
            a = lax.dot_general(qe, ke, (((1,), (1,)), ((), ())), preferred_element_type=F32)
            oi = oi + jnp.dot(a.astype(BF16), vb[:lo], preferred_element_type=F32)
        for s in range(R):
            rel = jnp.where(t_idx >= s, bi - bi[s:s + 1], NEG_INF)
            w = jnp.sum(qi * ki[s:s + 1] * jnp.exp(rel), axis=-1, keepdims=True)
            oi = oi + w * vi[s:s + 1]
        o_ref[lo:lo + R, :] = oi

    b_t = b.T
    b_last = b_t[:, C - 1:C]
    ke_t = (k.T * jnp.exp(b_last - b_t)).astype(BF16)
    s_ref[...] = jnp.exp(b_last) * state + jnp.dot(ke_t, vb, preferred_element_type=F32)


def gla_scan(z, lg, s0):
    bsz, t, _ = z.shape
    C = GLA_CHUNK
    kq = GLA_DK // GLA_DKH
    return pl.pallas_call(
        _gla_scan_kernel,
        grid=(bsz, GLA_HEADS, t // C),
        in_specs=[
            pl.BlockSpec((None, C, GLA_DKH), lambda b, h, c: (b, c, h)),
            pl.BlockSpec((None, C, GLA_DKH), lambda b, h, c: (b, c, kq + h)),
            pl.BlockSpec((None, C, GLA_DVH), lambda b, h, c: (b, c, (2 * GLA_DK) // GLA_DVH + h)),
            pl.BlockSpec((None, C, GLA_DKH), lambda b, h, c: (b, c, h)),
            pl.BlockSpec((None, None, GLA_DKH, GLA_DVH), lambda b, h, c: (b, h, 0, 0)),
        ],
        out_specs=[
            pl.BlockSpec((None, C, GLA_DVH), lambda b, h, c: (b, c, h)),
            pl.BlockSpec((None, None, GLA_DKH, GLA_DVH), lambda b, h, c: (b, h, 0, 0)),
        ],
        out_shape=[jax.ShapeDtypeStruct((bsz, t, GLA_DV), F32),
                   jax.ShapeDtypeStruct((bsz, GLA_HEADS, GLA_DKH, GLA_DVH), F32)],
        compiler_params=_params("arbitrary", "arbitrary", "arbitrary"),
        name="gla_scan",
    )(z, z, z, lg, s0)


def _gla_out_kernel(o_ref, g_ref, ng_ref, w_ref, x_ref, ga_ref, y_ref, p_scr):
    @pl.when(pl.program_id(1) == 0)
    def _():
        gate = _silu(g_ref[...])
        for h in range(GLA_HEADS):
            sl = slice(h * GLA_DVH, (h + 1) * GLA_DVH)
            o = o_ref[:, sl]
            ms = jnp.mean(o * o, axis=-1, keepdims=True)
            p_scr[:, sl] = ((o * lax.rsqrt(ms + RMS_EPS)) * ng_ref[...] * gate[:, sl]).astype(BF16)

    y_ref[...] = x_ref[...] + ga_ref[...] * jnp.dot(p_scr[...], w_ref[...], preferred_element_type=F32)


def _fox_out_kernel(o_ref, g_ref, w_ref, x_ref, ga_ref, y_ref, p_scr):
    @pl.when(pl.program_id(1) == 0)
    def _():
        p_scr[...] = (o_ref[...] * jax.nn.sigmoid(g_ref[...])).astype(BF16)

    y_ref[...] = x_ref[...] + ga_ref[...] * jnp.dot(p_scr[...], w_ref[...], preferred_element_type=F32)


def out_proj(o, z, g_block, norm_gain, w_out, x, mods, tm, rows_per_batch, tn=512):
    m = x.shape[0]
    row_spec = pl.BlockSpec((tm, D_MODEL), lambda i, j: (i, 0))
    in_specs = [row_spec, pl.BlockSpec((tm, D_MODEL), lambda i, j: (i, g_block))]
    args = [o, z]
    if norm_gain is not None:
        in_specs.append(pl.BlockSpec((1, GLA_DVH), lambda i, j: (0, 0)))
        args.append(norm_gain.reshape(1, GLA_DVH))
        body = _gla_out_kernel
    else:
        body = _fox_out_kernel
    in_specs += [pl.BlockSpec((D_MODEL, tn), lambda i, j: (0, j)),
                 pl.BlockSpec((tm, tn), lambda i, j: (i, j))]
    args += [w_out, x]
    if mods.ndim == 3:
        ga_spec = pl.BlockSpec((None, 1, tn),
                               lambda i, j: ((i * tm) // rows_per_batch, 0, 2 * (D_MODEL // tn) + j))
    else:
        ga_spec = pl.BlockSpec((tm, tn), lambda i, j: (i, 2 * (D_MODEL // tn) + j))
    in_specs.append(ga_spec)
    args.append(mods)
    return pl.pallas_call(
        body,
        grid=(m // tm, D_MODEL // tn),
        in_specs=in_specs,
        out_specs=pl.BlockSpec((tm, tn), lambda i, j: (i, j)),
        out_shape=jax.ShapeDtypeStruct((m, D_MODEL), F32),
        scratch_shapes=[pltpu.VMEM((tm, D_MODEL), BF16)],
        compiler_params=_params("arbitrary", "arbitrary"),
        name="out_proj",
    )(*args)


def _fox_in_kernel(x_ref, g_ref, sh_ref, sc_ref, w_ref, wf_ref, bf_ref, qn_ref, kn_ref, z_ref, lf_ref, h_scr,
                   *, tn):
    j = pl.program_id(1)

    @pl.when(j == 0)
    def _():
        h = _modulate(x_ref[...], g_ref[...], sh_ref[...], sc_ref[...]).astype(BF16)
        h_scr[...] = h
        f = jnp.dot(h, wf_ref[...], preferred_element_type=F32)
        lf_ref[...] = _log_sigmoid(f + bf_ref[...])

    acc = jnp.dot(h_scr[...], w_ref[...], preferred_element_type=F32)
    qk_tiles = FOX_DIM // tn

    @pl.when(j < 2 * qk_tiles)
    def _():
        gain = jnp.where(j < qk_tiles, qn_ref[...], kn_ref[...])
        for c in range(tn // FOX_HD):
            blk = acc[:, c * FOX_HD:(c + 1) * FOX_HD]
            ms = jnp.mean(blk * blk, axis=-1, keepdims=True)
            z_ref[:, c * FOX_HD:(c + 1) * FOX_HD] = (blk * lax.rsqrt(ms + RMS_EPS)) * gain

    @pl.when(j >= 2 * qk_tiles)
    def _():
        z_ref[...] = acc


def fox_in_proj(x, gain, mods, w_in, b_f, q_norm, k_norm, tm, rows_per_batch, tn=512):
    m = x.shape[0]
    n_main = 4 * FOX_DIM
    w_f = w_in[:, n_main:]
    sh_spec, sc_spec = _mod_specs(mods, (0, 1), tm, rows_per_batch)
    return pl.pallas_call(
        functools.partial(_fox_in_kernel, tn=tn),
        grid=(m // tm, n_main // tn),
        in_specs=[
            pl.BlockSpec((tm, D_MODEL), lambda i, j: (i, 0)),
            pl.BlockSpec((1, D_MODEL), lambda i, j: (0, 0)),
            sh_spec, sc_spec,
            pl.BlockSpec((D_MODEL, tn), lambda i, j: (0, j)),
            pl.BlockSpec((D_MODEL, FOX_HEADS), lambda i, j: (0, 0)),
            pl.BlockSpec((1, FOX_HEADS), lambda i, j: (0, 0)),
            pl.BlockSpec((1, FOX_HD), lambda i, j: (0, 0)),
            pl.BlockSpec((1, FOX_HD), lambda i, j: (0, 0)),
        ],
        out_specs=[
            pl.BlockSpec((tm, tn), lambda i, j: (i, j)),
            pl.BlockSpec((tm, FOX_HEADS), lambda i, j: (i, 0)),
        ],
        out_shape=[jax.ShapeDtypeStruct((m, n_main), F32), jax.ShapeDtypeStruct((m, FOX_HEADS), F32)],
        scratch_shapes=[pltpu.VMEM((tm, D_MODEL), BF16)],
        compiler_params=_params("arbitrary", "arbitrary"),
        name="fox_in_proj",
    )(x, gain.reshape(1, D_MODEL), mods, mods, w_in, w_f, b_f.reshape(1, FOX_HEADS),
      q_norm.reshape(1, FOX_HD), k_norm.reshape(1, FOX_HD))


def _cumsum_kernel(x_ref, o_ref):
    t = x_ref.shape[0]
    blk = LANES
    row = lax.broadcasted_iota(jnp.int32, (blk, blk), 0)
    col = lax.broadcasted_iota(jnp.int32, (blk, blk), 1)
    tri = (row >= col).astype(BF16)
    carry = jnp.zeros((1, x_ref.shape[1]), F32)
    for i in range(t // blk):
        c = _dot_exact_lhs(tri, x_ref[i * blk:(i + 1) * blk, :]) + carry
        o_ref[i * blk:(i + 1) * blk, :] = c
        carry = c[blk - 1:blk]


def cumsum_time(x):
    bsz, t, h = x.shape
    return pl.pallas_call(
        _cumsum_kernel,
        grid=(bsz,),
        in_specs=[pl.BlockSpec((None, t, h), lambda b: (b, 0, 0))],
        out_specs=pl.BlockSpec((None, t, h), lambda b: (b, 0, 0)),
        out_shape=jax.ShapeDtypeStruct((bsz, t, h), F32),
        compiler_params=_params("arbitrary"),
        name="cumsum_time",
    )(x)


def _fox_attn_kernel(q_ref, k_ref, v_ref, cc_ref, cr_ref, o_ref):
    blk = ATTN_BLOCK
    t = q_ref.shape[0]
    nt_dims = (((1,), (1,)), ((), ()))
    kb = k_ref[...].astype(BF16)
    vb = v_ref[...].astype(BF16)
    row = lax.broadcasted_iota(jnp.int32, (blk, blk), 0)
    col = lax.broadcasted_iota(jnp.int32, (blk, blk), 1)
    for i in range(t // blk):
        lo = i * blk
        q = (q_ref[lo:lo + blk, :] * (FOX_HD ** -0.5)).astype(BF16)
        cq = cc_ref[lo:lo + blk, :]
        s_d = lax.dot_general(q, kb[lo:lo + blk], nt_dims, preferred_element_type=F32)
        s_d = jnp.where(col <= row, s_d + cq - cr_ref[:, lo:lo + blk], NEG_INF)
        m = jnp.max(s_d, axis=-1, keepdims=True)
        if i > 0:
            s_p = lax.dot_general(q, kb[:lo], nt_dims, preferred_element_type=F32) + cq - cr_ref[:, :lo]
            m = jnp.maximum(m, jnp.max(s_p, axis=-1, keepdims=True))
        p_d = jnp.exp(s_d - m)
        l = jnp.sum(p_d, axis=-1, keepdims=True)
        acc = jnp.dot(p_d.astype(BF16), vb[lo:lo + blk], preferred_element_type=F32)
        if i > 0:
            p_p = jnp.exp(s_p - m)
            l = l + jnp.sum(p_p, axis=-1, keepdims=True)
            acc = acc + jnp.dot(p_p.astype(BF16), vb[:lo], preferred_element_type=F32)
        o_ref[lo:lo + blk, :] = acc / l


def fox_prompt_attn(z, cum):
    bsz, t, _ = z.shape
    cum_h = jnp.transpose(cum, (0, 2, 1))
    cum_col = cum_h.reshape(bsz, FOX_HEADS, t, 1)
    cum_row = cum_h.reshape(bsz, FOX_HEADS, 1, t)
    return pl.pallas_call(
        _fox_attn_kernel,
        grid=(bsz, FOX_HEADS),
        in_specs=[
            pl.BlockSpec((None, t, FOX_HD), lambda b, h: (b, 0, h)),
            pl.BlockSpec((None, t, FOX_HD), lambda b, h: (b, 0, FOX_HEADS + h)),
            pl.BlockSpec((None, t, FOX_HD), lambda b, h: (b, 0, 2 * FOX_HEADS + h)),
            pl.BlockSpec((None, None, t, 1), lambda b, h: (b, h, 0, 0)),
            pl.BlockSpec((None, None, 1, t), lambda b, h: (b, h, 0, 0)),
        ],
        out_specs=pl.BlockSpec((None, t, FOX_HD), lambda b, h: (b, 0, h)),
        out_shape=jax.ShapeDtypeStruct((bsz, t, FOX_DIM), F32),
        compiler_params=_params("arbitrary", "arbitrary"),
        name="fox_prompt_attn",
    )(z, z, z, cum_col, cum_row)


def _fox_suffix_kernel(pt_ref, *rest):
    G = SUFFIX_PAGES_PER_STEP
    lf_refs = rest[:G]
    o_ref, carry_scr = rest[G:]
    page = lf_refs[0].shape[0]

    @pl.when(pl.program_id(1) == 0)
    def _():
        carry_scr[...] = jnp.zeros(carry_scr.shape, F32)

    row = lax.broadcasted_iota(jnp.int32, (page, page), 0)
    col = lax.broadcasted_iota(jnp.int32, (page, page), 1)
    upper = (col > row).astype(BF16)
    carry = carry_scr[...]
    for g in range(G):
        lf = lf_refs[g][...]
        o_ref[G - 1 - g] = _dot_exact_lhs(upper, lf) + carry
        carry = carry + jnp.sum(lf, axis=0, keepdims=True)
    carry_scr[...] = carry


def fox_suffix(cache_logf, layer, page_table):
    db, n_pages = page_table.shape
    page = cache_logf.shape[2]
    G = SUFFIX_PAGES_PER_STEP
    steps = n_pages // G
    lf_specs = [pl.BlockSpec((None, None, page, FOX_HEADS),
                             lambda b, t, pt, g=g: (layer, pt[b, n_pages - 1 - (t * G + g)], 0, 0))
                for g in range(G)]
    grid_spec = pltpu.PrefetchScalarGridSpec(
        num_scalar_prefetch=1,
        grid=(db, steps),
        in_specs=lf_specs,
        out_specs=pl.BlockSpec((None, G, page, FOX_HEADS), lambda b, t, pt: (b, steps - 1 - t, 0, 0)),
        scratch_shapes=[pltpu.VMEM((1, FOX_HEADS), F32)],
    )
    return pl.pallas_call(
        _fox_suffix_kernel,
        grid_spec=grid_spec,
        out_shape=jax.ShapeDtypeStruct((db, n_pages, page, FOX_HEADS), F32),
        compiler_params=_params("arbitrary", "arbitrary"),
        name="fox_suffix",
    )(page_table, *([cache_logf] * G))


def _fox_paged_kernel(pt_ref, q_ref, ct_ref, kn_ref, vn_ref, bn_ref, suf_ref, *rest):
    G = PAGES_PER_STEP
    k_refs = rest[:G]
    v_refs = rest[G:2 * G]
    o_ref = rest[2 * G]
    m_scr, l_scr, acc_scr = rest[2 * G + 1:]
    t = pl.program_id(1)
    page = k_refs[0].shape[0]
    rows = q_ref.shape[0]
    nt_dims = (((1,), (1,)), ((), ()))

    @pl.when(t == 0)
    def _():
        m_scr[...] = jnp.full(m_scr.shape, NEG_INF, F32)
        l_scr[...] = jnp.zeros(l_scr.shape, F32)
        acc_scr[...] = jnp.zeros(acc_scr.shape, F32)

    q = q_ref[...].astype(BF16)
    r_id = lax.broadcasted_iota(jnp.int32, (rows, LANES), 0)
    c_id = lax.broadcasted_iota(jnp.int32, (rows, LANES), 1)
    head_bias = jnp.where(r_id % FOX_HEADS == c_id % FOX_HEADS, 0.0, NEG_INF)
    head_bias = jnp.concatenate([head_bias] * (page * FOX_HEADS // LANES), axis=1) + ct_ref[...]

    def online(carry, s_list, v_list):
        m, l, acc = carry
        m_new = m
        for s in s_list:
            m_new = jnp.maximum(m_new, jnp.max(s, axis=-1, keepdims=True))
        alpha = jnp.exp(m - m_new)
        l = alpha * l
        acc = alpha * acc
        for s, v2 in zip(s_list, v_list):
            p = jnp.exp(s - m_new)
            l = l + jnp.sum(p, axis=-1, keepdims=True)
            acc = acc + jnp.dot(p.astype(BF16), v2, preferred_element_type=F32)
        return m_new, l, acc

    s_list, v_list = [], []
    for g in range(G):
        k2 = k_refs[g][...].reshape(page * FOX_HEADS, FOX_HD).astype(BF16)
        v_list.append(v_refs[g][...].reshape(page * FOX_HEADS, FOX_HD).astype(BF16))
        s_list.append(lax.dot_general(q, k2, nt_dims, preferred_element_type=F32) + (head_bias + suf_ref[g]))
    carry = online((m_scr[...], l_scr[...], acc_scr[...]), s_list, v_list)
    m_scr[...], l_scr[...], acc_scr[...] = carry

    @pl.when(t == pl.num_programs(1) - 1)
    def _():
        s = lax.dot_general(q, kn_ref[...].astype(BF16), nt_dims, preferred_element_type=F32) + bn_ref[...]
        _, l, acc = online(carry, [s], [vn_ref[...].astype(BF16)])
        o_ref[...] = acc / l


def fox_paged_attn(q, k_new, v_new, logf_new, cache_k, cache_v, cache_logf, layer, page_table):
    db, s_len, _, _ = q.shape
    n_pages = page_table.shape[1]
    page = cache_k.shape[2]
    G = PAGES_PER_STEP
    rows = s_len * FOX_HEADS
    new_cols = LANES
    assert rows % 8 == 0 and rows <= new_cols and (page * FOX_HEADS) % LANES == 0

    suf = fox_suffix(cache_logf, layer, page_table).reshape(db, n_pages, 1, page * FOX_HEADS)

    q2 = (q * (FOX_HD ** -0.5)).reshape(db, rows, FOX_HD)
    cum = jnp.cumsum(logf_new, axis=1)
    ct = cum.reshape(db, rows, 1)
    same_head = jnp.arange(FOX_HEADS)[:, None] == jnp.arange(FOX_HEADS)[None, :]
    causal = jnp.arange(s_len)[None, :] <= jnp.arange(s_len)[:, None]
    ok = causal[:, None, :, None] & same_head[None, :, None, :]
    bn = jnp.where(ok[None], cum[:, :, :, None, None] - cum[:, None, None, :, :], NEG_INF)
    bn = jnp.pad(bn.reshape(db, rows, rows), ((0, 0), (0, 0), (0, new_cols - rows)), constant_values=NEG_INF)
    kn = jnp.pad(k_new.reshape(db, rows, FOX_HD), ((0, 0), (0, new_cols - rows), (0, 0)))
    vn = jnp.pad(v_new.reshape(db, rows, FOX_HD), ((0, 0), (0, new_cols - rows), (0, 0)))

    kv_specs = [pl.BlockSpec((None, None, page, FOX_HEADS, FOX_HD),
                             lambda b, t, pt, g=g: (layer, pt[b, t * G + g], 0, 0, 0)) for g in range(G)]
    grid_spec = pltpu.PrefetchScalarGridSpec(
        num_scalar_prefetch=1,
        grid=(db, n_pages // G),
        in_specs=[
            pl.BlockSpec((None, rows, FOX_HD), lambda b, t, pt: (b, 0, 0)),
            pl.BlockSpec((None, rows, 1), lambda b, t, pt: (b, 0, 0)),
            pl.BlockSpec((None, new_cols, FOX_HD), lambda b, t, pt: (b, 0, 0)),
            pl.BlockSpec((None, new_cols, FOX_HD), lambda b, t, pt: (b, 0, 0)),
            pl.BlockSpec((None, rows, new_cols), lambda b, t, pt: (b, 0, 0)),
            pl.BlockSpec((None, G, 1, page * FOX_HEADS), lambda b, t, pt: (b, t, 0, 0)),
        ] + kv_specs + kv_specs,
        out_specs=pl.BlockSpec((None, rows, FOX_HD), lambda b, t, pt: (b, 0, 0)),
        scratch_shapes=[pltpu.VMEM((rows, 1), F32), pltpu.VMEM((rows, 1), F32),
                        pltpu.VMEM((rows, FOX_HD), F32)],
    )
    o = pl.pallas_call(
        _fox_paged_kernel,
        grid_spec=grid_spec,
        out_shape=jax.ShapeDtypeStruct((db, rows, FOX_HD), F32),
        compiler_params=_params("arbitrary", "arbitrary"),
        name="fox_paged_attn",
    )(page_table, q2, ct, kn, vn, bn, suf, *([cache_k] * G), *([cache_v] * G))
    return o.reshape(db, s_len, FOX_HEADS, FOX_HD)


def _moe_pre_kernel(x_ref, g_ref, sh_ref, sc_ref, wr_ref, br_ref, h_ref, lo_ref):
    h = _modulate(x_ref[...], g_ref[...], sh_ref[...], sc_ref[...])
    h_ref[...] = h
    lo_ref[...] = jnp.dot(h.astype(BF16), wr_ref[...], preferred_element_type=F32) + br_ref[...]


def moe_pre(x, gain, mods, w_router, b_router, tm, rows_per_batch):
    m = x.shape[0]
    sh_spec, sc_spec = _mod_specs(mods, (3, 4), tm, rows_per_batch)
    return pl.pallas_call(
        _moe_pre_kernel,
        grid=(m // tm, 1),
        in_specs=[
            pl.BlockSpec((tm, D_MODEL), lambda i, j: (i, 0)),
            pl.BlockSpec((1, D_MODEL), lambda i, j: (0, 0)),
            sh_spec, sc_spec,
            pl.BlockSpec((D_MODEL, ROUTER_COLS), lambda i, j: (0, 0)),
            pl.BlockSpec((1, ROUTER_COLS), lambda i, j: (0, 0)),
        ],
        out_specs=[
            pl.BlockSpec((tm, D_MODEL), lambda i, j: (i, 0)),
            pl.BlockSpec((tm, ROUTER_COLS), lambda i, j: (i, 0)),
        ],
        out_shape=[jax.ShapeDtypeStruct((m, D_MODEL), F32), jax.ShapeDtypeStruct((m, ROUTER_COLS), F32)],
        compiler_params=_params("arbitrary", "arbitrary"),
        name="moe_pre",
    )(x, gain.reshape(1, D_MODEL), mods, mods, w_router, b_router)


def _moe_expert_kernel(be_ref, nu_ref, tok_ref, h_hbm, wg_ref, wu_ref, wd_ref, y_ref,
                       x_buf, sem, wg_s, wu_s, wd_s):
    i = pl.program_id(0)
    tb = x_buf.shape[1]
    n_used = nu_ref[0]

    def row_copy(blk, r, half):
        return pltpu.make_async_copy(h_hbm.at[pl.ds(tok_ref[blk * tb + r], 1)],
                                     x_buf.at[half, pl.ds(r, 1)], sem.at[half])

    def start_rows(blk):
        def body(r, c):
            row_copy(blk, r, blk % 2).start()
            return c
        lax.fori_loop(0, tb, body, 0, unroll=8)

    def wait_rows(blk):
        def body(r, c):
            row_copy(blk, r, blk % 2).wait()
            return c
        lax.fori_loop(0, tb, body, 0, unroll=8)

    @pl.when((i == 0) & (n_used > 0))
    def _():
        start_rows(0)

    @pl.when(i + 1 < n_used)
    def _():
        start_rows(i + 1)

    prev = be_ref[jnp.maximum(i - 1, 0)]

    @pl.when((i == 0) | (be_ref[i] != prev))
    def _():
        wg_s[...] = wg_ref[...].astype(BF16)
        wu_s[...] = wu_ref[...].astype(BF16)
        wd_s[...] = wd_ref[...].astype(BF16)

    @pl.when(i < n_used)
    def _():
        wait_rows(i)
        x = x_buf[i % 2].astype(BF16)
        a = jnp.dot(x, wg_s[...], preferred_element_type=F32)
        u = jnp.dot(x, wu_s[...], preferred_element_type=F32)
        y_ref[...] = jnp.dot((_silu(a) * u).astype(BF16), wd_s[...], preferred_element_type=F32)

    @pl.when(i >= n_used)
    def _():
        y_ref[...] = jnp.zeros(y_ref.shape, F32)


def moe_experts(h_all, buf_tok, block_e, n_used, w_gate, w_up, w_down, layer):
    cap = buf_tok.shape[0]
    tb = MOE_BLOCK
    grid_spec = pltpu.PrefetchScalarGridSpec(
        num_scalar_prefetch=3,
        grid=(cap // tb,),
        in_specs=[
            pl.BlockSpec(memory_space=pl.ANY),
            pl.BlockSpec((None, None, D_MODEL, MOE_DFF), lambda i, be, nu, tok: (layer, be[i], 0, 0)),
            pl.BlockSpec((None, None, D_MODEL, MOE_DFF), lambda i, be, nu, tok: (layer, be[i], 0, 0)),
            pl.BlockSpec((None, None, MOE_DFF, D_MODEL), lambda i, be, nu, tok: (layer, be[i], 0, 0)),
        ],
        out_specs=pl.BlockSpec((tb, D_MODEL), lambda i, be, nu, tok: (i, 0)),
        scratch_shapes=[pltpu.VMEM((2, tb, D_MODEL), F32), pltpu.SemaphoreType.DMA((2,)),
                        pltpu.VMEM((D_MODEL, MOE_DFF), BF16), pltpu.VMEM((D_MODEL, MOE_DFF), BF16),
                        pltpu.VMEM((MOE_DFF, D_MODEL), BF16)],
    )
    return pl.pallas_call(
        _moe_expert_kernel,
        grid_spec=grid_spec,
        out_shape=jax.ShapeDtypeStruct((cap, D_MODEL), F32),
        compiler_params=_params("arbitrary"),
        name="moe_experts",
    )(block_e, n_used, buf_tok, h_all, w_gate, w_up, w_down)


def _route_kernel(lo_ref, eid_ref, gw_ref, rank_ref, cnt_ref, carry_scr, *, n_valid):
    tm = lo_ref.shape[0]
    i = pl.program_id(0)

    @pl.when(i == 0)
    def _():
        carry_scr[...] = jnp.zeros(carry_scr.shape, F32)

    lo = lo_ref[...]
    lane_i = lax.broadcasted_iota(jnp.int32, lo.shape, 1)
    lane = lane_i.astype(F32)
    row = lax.broadcasted_iota(jnp.int32, lo.shape, 0) + i * tm

    def masked_softmax(mask):
        x = jnp.where(mask, lo, NEG_INF)
        e = jnp.where(mask, jnp.exp(x - jnp.max(x, axis=-1, keepdims=True)), 0.0)
        return e / jnp.sum(e, axis=-1, keepdims=True)

    def first_max(p, mask):
        pm = jnp.where(mask, p, -1.0)
        top = jnp.max(pm, axis=-1, keepdims=True)
        idx = jnp.min(jnp.where(pm == top, lane, float(LANES)), axis=-1, keepdims=True)
        return top, idx

    gmask = lane_i < MOE_GROUPS
    p_g, g_idx = first_max(masked_softmax(gmask), gmask)
    e_lane = lane_i - MOE_GROUPS
    emask = (e_lane >= 0) & (e_lane < MOE_EXPERTS) & ((e_lane // MOE_PER_GROUP).astype(F32) == g_idx)
    pe = masked_softmax(emask)
    w0, i0 = first_max(pe, emask)
    w1, i1 = first_max(pe, emask & (lane != i0))
    tw = w0 + w1
    two = lax.broadcasted_iota(jnp.int32, (tm, MOE_TOPK), 1)
    gw_ref[...] = jnp.where(two == 0, p_g * (w0 / tw), p_g * (w1 / tw))
    eid_ref[...] = jnp.where(two == 0, i0, i1).astype(jnp.int32) - MOE_GROUPS

    cnt = jnp.where((row < n_valid) & ((lane == i0) | (lane == i1)), 1.0, 0.0)
    r_id = lax.broadcasted_iota(jnp.int32, (tm, tm), 0)
    c_id = lax.broadcasted_iota(jnp.int32, (tm, tm), 1)
    earlier = (c_id < r_id).astype(BF16)
    before = jnp.dot(earlier, cnt.astype(BF16), preferred_element_type=F32) + carry_scr[...]
    rank0 = jnp.sum(jnp.where(lane == i0, before, 0.0), axis=-1, keepdims=True)
    rank1 = jnp.sum(jnp.where(lane == i1, before, 0.0), axis=-1, keepdims=True)
    rank_ref[...] = jnp.where(two == 0, rank0, rank1).astype(jnp.int32)
    total = carry_scr[...] + jnp.sum(cnt, axis=0, keepdims=True)
    carry_scr[...] = total
    cnt_ref[...] = total


def moe_route(logits, n_valid):
    n_pad = logits.shape[0]
    tm = ROUTE_BLOCK
    pair = lambda dt: jax.ShapeDtypeStruct((n_pad, MOE_TOPK), dt)
    pair_spec = pl.BlockSpec((tm, MOE_TOPK), lambda i: (i, 0))
    return pl.pallas_call(
        functools.partial(_route_kernel, n_valid=n_valid),
        grid=(n_pad // tm,),
        in_specs=[pl.BlockSpec((tm, ROUTER_COLS), lambda i: (i, 0))],
        out_specs=[pair_spec, pair_spec, pair_spec, pl.BlockSpec((1, ROUTER_COLS), lambda i: (0, 0))],
        out_shape=[pair(jnp.int32), pair(F32), pair(jnp.int32), jax.ShapeDtypeStruct((1, ROUTER_COLS), F32)],
        scratch_shapes=[pltpu.VMEM((1, ROUTER_COLS), F32)],
        compiler_params=_params("arbitrary"),
        name="moe_route",
    )(logits)


def _dispatch(e_ids, rank, lane_counts):
    n = e_ids.shape[0]
    tb = MOE_BLOCK
    counts = lane_counts[0, MOE_GROUPS:MOE_GROUPS + MOE_EXPERTS].astype(jnp.int32)
    padded = ((counts + tb - 1) // tb) * tb
    pend = jnp.cumsum(padded)
    pstart = pend - padded
    dest = pstart[e_ids] + rank
    n_blocks = (n * MOE_TOPK + MOE_EXPERTS * (tb - 1) + tb - 1) // tb
    tok = jnp.repeat(jnp.arange(n, dtype=jnp.int32), MOE_TOPK)
    buf_tok = jnp.zeros((n_blocks * tb,), jnp.int32).at[dest.reshape(-1)].set(tok)
    block_e = jnp.minimum(jnp.searchsorted(pend, jnp.arange(n_blocks, dtype=jnp.int32) * tb, side='right'),
                          MOE_EXPERTS - 1).astype(jnp.int32)
    n_used = (pend[-1] // tb).astype(jnp.int32).reshape(1)
    return dest, buf_tok, block_e, n_used


def _combine_kernel(slot_ref, x_ref, ga_ref, w_ref, y_hbm, *rest, tok0, final):
    if final:
        g_ref, o_ref, y_buf, sem = rest
    else:
        o_ref, y_buf, sem = rest
    tm = x_ref.shape[0]
    base = (tok0 + pl.program_id(0) * tm) * MOE_TOPK

    def row_copy(r, k):
        return pltpu.make_async_copy(y_hbm.at[pl.ds(slot_ref[base + r * MOE_TOPK + k], 1)],
                                     y_buf.at[k, pl.ds(r, 1)], sem.at[k])

    def start(r, c):
        for k in range(MOE_TOPK):
            row_copy(r, k).start()
        return c

    def wait(r, c):
        for k in range(MOE_TOPK):
            row_copy(r, k).wait()
        return c

    lax.fori_loop(0, tm, start, 0, unroll=8)
    lax.fori_loop(0, tm, wait, 0, unroll=8)
    w = w_ref[...]
    out = x_ref[...] + ga_ref[...] * (y_buf[0] * w[:, 0:1] + y_buf[1] * w[:, 1:2])
    if final:
        ms = jnp.mean(out * out, axis=-1, keepdims=True)
        out = (out * lax.rsqrt(ms + RMS_EPS)) * g_ref[...]
    o_ref[...] = out


def moe_combine(x, mods, y, slot_flat, gate_w, tok0, tm, rows_per_batch, final_gain=None):
    m = x.shape[0]
    final = final_gain is not None
    if mods.ndim == 3:
        ga_spec = pl.BlockSpec((None, 1, D_MODEL), lambda i, s: ((i * tm) // rows_per_batch, 0, 5))
    else:
        ga_spec = pl.BlockSpec((tm, D_MODEL), lambda i, s: (i, 5))
    row_spec = pl.BlockSpec((tm, D_MODEL), lambda i, s: (i, 0))
    in_specs = [row_spec, ga_spec,
                pl.BlockSpec((tm, MOE_TOPK), lambda i, s: (tok0 // tm + i, 0)),
                pl.BlockSpec(memory_space=pl.ANY)]
    args = [x, mods, gate_w, y]
    if final:
        in_specs.append(pl.BlockSpec((1, D_MODEL), lambda i, s: (0, 0)))
        args.append(final_gain.reshape(1, D_MODEL))
    grid_spec = pltpu.PrefetchScalarGridSpec(
        num_scalar_prefetch=1,
        grid=(m // tm,),
        in_specs=in_specs,
        out_specs=row_spec,
        scratch_shapes=[pltpu.VMEM((MOE_TOPK, tm, D_MODEL), F32), pltpu.SemaphoreType.DMA((MOE_TOPK,))],
    )
    return pl.pallas_call(
        functools.partial(_combine_kernel, tok0=tok0, final=final),
        grid_spec=grid_spec,
        out_shape=jax.ShapeDtypeStruct((m, D_MODEL), F32),
        compiler_params=_params("arbitrary"),
        name="moe_combine",
    )(slot_flat, *args)


def kernel(x_prompt, x_sample, c_prompt, c_sample, state_gla, cache_k, cache_v, cache_logf, page_table, norm_mix, norm_ffn, norm_final, w_ada, b_ada, gla_w_in, gla_w_gate_up, gla_b_gate, gla_norm, gla_w_out, fox_w_in, fox_b_f, fox_q_norm, fox_k_norm, fox_w_out, moe_w_group, moe_b_group, moe_w_expert, moe_b_expert, moe_w_gate, moe_w_up, moe_w_down):
    bsz, seq, d = x_prompt.shape
    db, ds, _ = x_sample.shape
    depth = w_ada.shape[0]
    mp, msz = bsz * seq, db * ds
    ti_p = min(1024, seq)
    tm_p = min(512, seq)
    te_p = min(256, seq)
    n_tok = mp + msz
    n_route = -(-n_tok // ROUTE_BLOCK) * ROUTE_BLOCK
    assert d == D_MODEL and seq % ti_p == 0 and seq % GLA_CHUNK == 0 and seq % ATTN_BLOCK == 0
    assert ds <= GLA_CHUNK and page_table.shape[1] % PAGES_PER_STEP == 0
    assert page_table.shape[1] % SUFFIX_PAGES_PER_STEP == 0

    gla_w_in_b, gla_w_up_b, gla_w_out_b = (w.astype(BF16) for w in (gla_w_in, gla_w_gate_up, gla_w_out))
    fox_w_in_b, fox_w_out_b = fox_w_in.astype(BF16), fox_w_out.astype(BF16)

    c_rows = jnp.concatenate([c_prompt, c_sample], axis=0)
    c_rows = jnp.pad(c_rows, ((0, (-c_rows.shape[0]) % 8), (0, 0)))
    ada = ada_all(c_rows, w_ada, b_ada)

    xp = x_prompt.reshape(mp, d)
    xs = x_sample.reshape(msz, d)
    gla_p, gla_s, kp_l, vp_l, lfp_l, ks_l, vs_l, lfs_l = [], [], [], [], [], [], [], []
    for i in range(depth):
        j = i // 2
        mods_p = ada[i, :bsz].reshape(bsz, 1, 6 * d)
        mods_s = jnp.repeat(ada[i, bsz:bsz + db], ds, axis=0)
        if i % 2 == 0:
            zp, lgp = gla_in_proj(xp, norm_mix[i], mods_p, gla_w_in_b[j], gla_w_up_b[j], gla_b_gate[j],
                                  ti_p, seq)
            s0 = jnp.zeros((bsz, GLA_HEADS, GLA_DKH, GLA_DVH), F32)
            op, s_fin = gla_scan(zp.reshape(bsz, seq, GLA_MAIN), lgp.reshape(bsz, seq, GLA_DK), s0)
            xp = out_proj(op.reshape(mp, d), zp, 2, gla_norm[j], gla_w_out_b[j], xp, mods_p, tm_p, seq)
            gla_p.append(s_fin)

            zs, lgs = gla_in_proj(xs, norm_mix[i], mods_s, gla_w_in_b[j], gla_w_up_b[j], gla_b_gate[j],
                                  msz, msz)
            pad = ((0, 0), (0, GLA_CHUNK - ds), (0, 0))
            zs_pad = jnp.pad(zs.reshape(db, ds, GLA_MAIN), pad)
            lgs_pad = jnp.pad(lgs.reshape(db, ds, GLA_DK), pad)
            os_pad, s_new = gla_scan(zs_pad, lgs_pad, state_gla[j])
            xs = out_proj(os_pad[:, :ds].reshape(msz, d), zs, 2, gla_norm[j], gla_w_out_b[j], xs, mods_s,
                          msz, msz)
            gla_s.append(s_new)
        else:
            zp, lfp = fox_in_proj(xp, norm_mix[i], mods_p, fox_w_in_b[j], fox_b_f[j], fox_q_norm[j],
                                  fox_k_norm[j], ti_p, seq)
            z3 = zp.reshape(bsz, seq, 4 * FOX_DIM)
            lf3 = lfp.reshape(bsz, seq, FOX_HEADS)
            op = fox_prompt_attn(z3, cumsum_time(lf3))
            xp = out_proj(op.reshape(mp, d), zp, 3, None, fox_w_out_b[j], xp, mods_p, tm_p, seq)
            kp_l.append(z3[:, :, FOX_DIM:2 * FOX_DIM].reshape(bsz, seq, FOX_HEADS, FOX_HD))
            vp_l.append(z3[:, :, 2 * FOX_DIM:3 * FOX_DIM].reshape(bsz, seq, FOX_HEADS, FOX_HD))
            lfp_l.append(lf3)

            zs, lfs = fox_in_proj(xs, norm_mix[i], mods_s, fox_w_in_b[j], fox_b_f[j], fox_q_norm[j],
                                  fox_k_norm[j], msz, msz)
            zs4 = zs.reshape(db, ds, 4, FOX_HEADS, FOX_HD)
            q_s, k_s, v_s = zs4[:, :, 0], zs4[:, :, 1], zs4[:, :, 2]
            lfs3 = lfs.reshape(db, ds, FOX_HEADS)
            os_ = fox_paged_attn(q_s, k_s, v_s, lfs3, cache_k, cache_v, cache_logf, j, page_table)
            xs = out_proj(os_.reshape(msz, d), zs, 3, None, fox_w_out_b[j], xs, mods_s, msz, msz)
            ks_l.append(k_s)
            vs_l.append(v_s)
            lfs_l.append(lfs3)

        w_router = jnp.pad(jnp.concatenate([moe_w_group[i], moe_w_expert[i]], axis=1),
                           ((0, 0), (0, ROUTER_COLS - MOE_GROUPS - MOE_EXPERTS))).astype(BF16)
        b_router = jnp.pad(jnp.concatenate([moe_b_group[i], moe_b_expert[i]]),
                           (0, ROUTER_COLS - MOE_GROUPS - MOE_EXPERTS)).reshape(1, ROUTER_COLS)
        hp, lop = moe_pre(xp, norm_ffn[i], mods_p, w_router, b_router, tm_p, seq)
        hs, los = moe_pre(xs, norm_ffn[i], mods_s, w_router, b_router, msz, msz)
        h_all = jnp.concatenate([hp, hs], axis=0)
        logits = jnp.concatenate([lop, los, jnp.zeros((n_route - n_tok, ROUTER_COLS), F32)], axis=0)
        e_ids, gate_w, rank, lane_counts = moe_route(logits, n_tok)
        slot, buf_tok, block_e, n_used = _dispatch(e_ids[:n_tok], rank[:n_tok], lane_counts)
        y = moe_experts(h_all, buf_tok, block_e, n_used, moe_w_gate, moe_w_up, moe_w_down, i)
        slot_flat = slot.reshape(-1)
        closing = norm_final if i == depth - 1 else None
        xp = moe_combine(xp, mods_p, y, slot_flat, gate_w, 0, te_p, seq, closing)
        xs = moe_combine(xs, mods_s, y, slot_flat, gate_w, mp, msz, msz, closing)

    y_prompt = xp.reshape(bsz, seq, d)
    y_sample = xs.reshape(db, ds, d)
    return (y_prompt, y_sample,
            jnp.stack(kp_l), jnp.stack(vp_l), jnp.stack(lfp_l), jnp.stack(gla_p),
            jnp.stack(ks_l), jnp.stack(vs_l), jnp.stack(lfs_l), jnp.stack(gla_s))
```

```python
import functools

import jax
import jax.numpy as jnp
from jax import lax
from jax.experimental import pallas as pl
from jax.experimental.pallas import tpu as pltpu

F32 = jnp.float32
BF16 = jnp.bfloat16

D_MODEL = 2048
GLA_HEADS = 4
GLA_DK = D_MODEL // 2
GLA_DV = D_MODEL
GLA_DKH = GLA_DK // GLA_HEADS
GLA_DVH = GLA_DV // GLA_HEADS
GLA_GATE_RANK = 16
GLA_GATE_NORM = 16.0
GLA_MAIN = 2 * GLA_DK + 2 * GLA_DV
FOX_HEADS = 16
FOX_HD = D_MODEL // FOX_HEADS
FOX_DIM = FOX_HEADS * FOX_HD
MOE_GROUPS = 4
MOE_PER_GROUP = 8
MOE_EXPERTS = MOE_GROUPS * MOE_PER_GROUP
MOE_TOPK = 2
MOE_DFF = D_MODEL // 4
RMS_EPS = 1e-6
NEG_INF = -1e30

VMEM_LIMIT_BYTES = 52 * 1024 * 1024
LANES = 128

GLA_CHUNK = 128
GLA_SUB = 16
ATTN_BLOCK = 256
PAGES_PER_STEP = 4
SUFFIX_PAGES_PER_STEP = 16
ROUTE_BLOCK = 256
MOE_BLOCK = 256
ROUTER_COLS = 128


def _params(*sem):
    return pltpu.CompilerParams(dimension_semantics=sem, vmem_limit_bytes=VMEM_LIMIT_BYTES)


def _log_sigmoid(x):
    return jnp.minimum(x, 0.0) - jnp.log1p(jnp.exp(-jnp.abs(x)))


def _silu(x):
    return x * jax.nn.sigmoid(x)


def _split3(a):
    hi = a.astype(BF16)
    r1 = a - hi.astype(F32)
    mid = r1.astype(BF16)
    lo = (r1 - mid.astype(F32)).astype(BF16)
    return hi, mid, lo


def _dot_exact_lhs(sel_bf16, x_f32):
    hi, mid, lo = _split3(x_f32)
    d = functools.partial(jnp.dot, preferred_element_type=F32)
    return d(sel_bf16, hi) + d(sel_bf16, mid) + d(sel_bf16, lo)


def _modulate(x, g, shift, scale):
    ms = jnp.mean(x * x, axis=-1, keepdims=True)
    return (x * lax.rsqrt(ms + RMS_EPS)) * g * (1.0 + scale) + shift


def _ada_kernel(c_ref, w_ref, b_ref, o_ref):
    a = _silu(c_ref[...]).astype(BF16)
    o_ref[...] = jnp.dot(a, w_ref[...].astype(BF16), preferred_element_type=F32) + b_ref[...]


def ada_all(c_rows, w_ada, b_ada, tn=1024):
    depth, d, n = w_ada.shape
    rows = c_rows.shape[0]
    return pl.pallas_call(
        _ada_kernel,
        grid=(depth, n // tn),
        in_specs=[
            pl.BlockSpec((rows, d), lambda l, j: (0, 0)),
            pl.BlockSpec((None, d, tn), lambda l, j: (l, 0, j)),
            pl.BlockSpec((None, 1, tn), lambda l, j: (l, 0, j)),
        ],
        out_specs=pl.BlockSpec((None, rows, tn), lambda l, j: (l, 0, j)),
        out_shape=jax.ShapeDtypeStruct((depth, rows, n), F32),
        compiler_params=_params("arbitrary", "arbitrary"),
        name="ada_all",
    )(c_rows, w_ada, b_ada.reshape(depth, 1, n))


def _mod_specs(mods, chunk_ids, tm, rows_per_batch):
    specs = []
    for c in chunk_ids:
        if mods.ndim == 3:
            specs.append(pl.BlockSpec((None, 1, D_MODEL),
                                      lambda i, j, c=c: ((i * tm) // rows_per_batch, 0, c)))
        else:
            specs.append(pl.BlockSpec((tm, D_MODEL), lambda i, j, c=c: (i, c)))
    return specs


def _gla_in_kernel(x_ref, g_ref, sh_ref, sc_ref, w_ref, wgd_ref, wup_ref, bg_ref, z_ref, lg_ref, h_scr):
    @pl.when(pl.program_id(1) == 0)
    def _():
        h = _modulate(x_ref[...], g_ref[...], sh_ref[...], sc_ref[...]).astype(BF16)
        h_scr[...] = h
        gd = jnp.dot(h, wgd_ref[...].astype(BF16), preferred_element_type=F32)
        gate = jnp.dot(gd.astype(BF16), wup_ref[...].astype(BF16), preferred_element_type=F32) + bg_ref[...]
        lg_ref[...] = _log_sigmoid(gate) * (1.0 / GLA_GATE_NORM)

    z_ref[...] = jnp.dot(h_scr[...], w_ref[...].astype(BF16), preferred_element_type=F32)


def gla_in_proj(x, gain, mods, w_in, w_gate_up, b_gate, tm, rows_per_batch, tn=512):
    m = x.shape[0]
    w_gd = w_in[:, GLA_MAIN:]
    sh_spec, sc_spec = _mod_specs(mods, (0, 1), tm, rows_per_batch)
    return pl.pallas_call(
        _gla_in_kernel,
        grid=(m // tm, GLA_MAIN // tn),
        in_specs=[
            pl.BlockSpec((tm, D_MODEL), lambda i, j: (i, 0)),
            pl.BlockSpec((1, D_MODEL), lambda i, j: (0, 0)),
            sh_spec, sc_spec,
            pl.BlockSpec((D_MODEL, tn), lambda i, j: (0, j)),
            pl.BlockSpec((D_MODEL, GLA_GATE_RANK), lambda i, j: (0, 0)),
            pl.BlockSpec((GLA_GATE_RANK, GLA_DK), lambda i, j: (0, 0)),
            pl.BlockSpec((1, GLA_DK), lambda i, j: (0, 0)),
        ],
        out_specs=[
            pl.BlockSpec((tm, tn), lambda i, j: (i, j)),
            pl.BlockSpec((tm, GLA_DK), lambda i, j: (i, 0)),
        ],
        out_shape=[jax.ShapeDtypeStruct((m, GLA_MAIN), F32), jax.ShapeDtypeStruct((m, GLA_DK), F32)],
        scratch_shapes=[pltpu.VMEM((tm, D_MODEL), BF16)],
        compiler_params=_params("arbitrary", "arbitrary"),
        name="gla_in_proj",
    )(x, gain.reshape(1, D_MODEL), mods, mods, w_in, w_gd, w_gate_up, b_gate.reshape(1, GLA_DK))


def _gla_scan_kernel(q_ref, k_ref, v_ref, lg_ref, s0_ref, o_ref, s_ref):
    C, R = GLA_CHUNK, GLA_SUB

    @pl.when(pl.program_id(2) == 0)
    def _():
        s_ref[...] = s0_ref[...]

    q = q_ref[...] * (GLA_DKH ** -0.5)
    k = k_ref[...]
    v = v_ref[...]
    vb = v.astype(BF16)
    row = lax.broadcasted_iota(jnp.int32, (C, C), 0)
    col = lax.broadcasted_iota(jnp.int32, (C, C), 1)
    tri = (row >= col).astype(BF16)
    b = _dot_exact_lhs(tri, lg_ref[...])
    state = s_ref[...]
    inter = jnp.dot((q * jnp.exp(b)).astype(BF16), state.astype(BF16), preferred_element_type=F32)

    t_idx = lax.broadcasted_iota(jnp.int32, (R, 1), 0)
    for i in range(C // R):
        lo = i * R
        bi = b[lo:lo + R]
        qi = q[lo:lo + R]
        ki = k[lo:lo + R]
        vi = v[lo:lo + R]
        oi = inter[lo:lo + R]
        if i > 0:
            b_ref_row = b[lo - 1:lo]
            qe = (qi * jnp.exp(bi - b_ref_row)).astype(BF16)
            ke = (k[:lo] * jnp.exp(b_ref_row - b[:lo])).astype(BF16)
            a = lax.dot_general(qe, ke, (((1,), (1,)), ((), ())), preferred_element_type=F32)
            oi = oi + jnp.dot(a.astype(BF16), vb[:lo], preferred_element_type=F32)
        for s in range(R):
            rel = jnp.where(t_idx >= s, bi - bi[s:s + 1], NEG_INF)
            w = jnp.sum(qi * ki[s:s + 1] * jnp.exp(rel), axis=-1, keepdims=True)
            oi = oi + w * vi[s:s + 1]
        o_ref[lo:lo + R, :] = oi

    b_t = b.T
    b_last = b_t[:, C - 1:C]
    ke_t = (k.T * jnp.exp(b_last - b_t)).astype(BF16)
    s_ref[...] = jnp.exp(b_last) * state + jnp.dot(ke_t, vb, preferred_element_type=F32)


def gla_scan(z, lg, s0):
    bsz, t, _ = z.shape
    C = GLA_CHUNK
    kq = GLA_DK // GLA_DKH
    return pl.pallas_call(
        _gla_scan_kernel,
        grid=(bsz, GLA_HEADS, t // C),
        in_specs=[
            pl.BlockSpec((None, C, GLA_DKH), lambda b, h, c: (b, c, h)),
            pl.BlockSpec((None, C, GLA_DKH), lambda b, h, c: (b, c, kq + h)),
            pl.BlockSpec((None, C, GLA_DVH), lambda b, h, c: (b, c, (2 * GLA_DK) // GLA_DVH + h)),
            pl.BlockSpec((None, C, GLA_DKH), lambda b, h, c: (b, c, h)),
            pl.BlockSpec((None, None, GLA_DKH, GLA_DVH), lambda b, h, c: (b, h, 0, 0)),
        ],
        out_specs=[
            pl.BlockSpec((None, C, GLA_DVH), lambda b, h, c: (b, c, h)),
            pl.BlockSpec((None, None, GLA_DKH, GLA_DVH), lambda b, h, c: (b, h, 0, 0)),
        ],
        out_shape=[jax.ShapeDtypeStruct((bsz, t, GLA_DV), F32),
                   jax.ShapeDtypeStruct((bsz, GLA_HEADS, GLA_DKH, GLA_DVH), F32)],
        compiler_params=_params("arbitrary", "arbitrary", "arbitrary"),
        name="gla_scan",
    )(z, z, z, lg, s0)


def _gla_out_kernel(o_ref, g_ref, ng_ref, w_ref, x_ref, ga_ref, y_ref, p_scr):
    @pl.when(pl.program_id(1) == 0)
    def _():
        gate = _silu(g_ref[...])
        for h in range(GLA_HEADS):
            sl = slice(h * GLA_DVH, (h + 1) * GLA_DVH)
            o = o_ref[:, sl]
            ms = jnp.mean(o * o, axis=-1, keepdims=True)
            p_scr[:, sl] = ((o * lax.rsqrt(ms + RMS_EPS)) * ng_ref[...] * gate[:, sl]).astype(BF16)

    y_ref[...] = x_ref[...] + ga_ref[...] * jnp.dot(p_scr[...], w_ref[...].astype(BF16),
                                                    preferred_element_type=F32)


def _fox_out_kernel(o_ref, g_ref, w_ref, x_ref, ga_ref, y_ref, p_scr):
    @pl.when(pl.program_id(1) == 0)
    def _():
        p_scr[...] = (o_ref[...] * jax.nn.sigmoid(g_ref[...])).astype(BF16)

    y_ref[...] = x_ref[...] + ga_ref[...] * jnp.dot(p_scr[...], w_ref[...].astype(BF16),
                                                    preferred_element_type=F32)


def out_proj(o, z, g_block, norm_gain, w_out, x, mods, tm, rows_per_batch, tn=512):
    m = x.shape[0]
    row_spec = pl.BlockSpec((tm, D_MODEL), lambda i, j: (i, 0))
    in_specs = [row_spec, pl.BlockSpec((tm, D_MODEL), lambda i, j: (i, g_block))]
    args = [o, z]
    if norm_gain is not None:
        in_specs.append(pl.BlockSpec((1, GLA_DVH), lambda i, j: (0, 0)))
        args.append(norm_gain.reshape(1, GLA_DVH))
        body = _gla_out_kernel
    else:
        body = _fox_out_kernel
    in_specs += [pl.BlockSpec((D_MODEL, tn), lambda i, j: (0, j)),
                 pl.BlockSpec((tm, tn), lambda i, j: (i, j))]
    args += [w_out, x]
    if mods.ndim == 3:
        ga_spec = pl.BlockSpec((None, 1, tn),
                               lambda i, j: ((i * tm) // rows_per_batch, 0, 2 * (D_MODEL // tn) + j))
    else:
        ga_spec = pl.BlockSpec((tm, tn), lambda i, j: (i, 2 * (D_MODEL // tn) + j))
    in_specs.append(ga_spec)
    args.append(mods)
    return pl.pallas_call(
        body,
        grid=(m // tm, D_MODEL // tn),
        in_specs=in_specs,
        out_specs=pl.BlockSpec((tm, tn), lambda i, j: (i, j)),
        out_shape=jax.ShapeDtypeStruct((m, D_MODEL), F32),
        scratch_shapes=[pltpu.VMEM((tm, D_MODEL), BF16)],
        compiler_params=_params("arbitrary", "arbitrary"),
        name="out_proj",
    )(*args)


def _fox_in_kernel(x_ref, g_ref, sh_ref, sc_ref, w_ref, wf_ref, bf_ref, qn_ref, kn_ref,
                   q_ref, k_ref, v_ref, go_ref, lf_ref, h_scr, *, tn):
    j = pl.program_id(1)
    nt = FOX_DIM // tn

    @pl.when(j == 0)
    def _():
        h = _modulate(x_ref[...], g_ref[...], sh_ref[...], sc_ref[...]).astype(BF16)
        h_scr[...] = h
        f = jnp.dot(h, wf_ref[...].astype(BF16), preferred_element_type=F32)
        lf_ref[...] = _log_sigmoid(f + bf_ref[...])

    acc = jnp.dot(h_scr[...], w_ref[...].astype(BF16), preferred_element_type=F32)

    def head_norm(o_ref, gain):
        for c in range(tn // FOX_HD):
            blk = acc[:, c * FOX_HD:(c + 1) * FOX_HD]
            ms = jnp.mean(blk * blk, axis=-1, keepdims=True)
            o_ref[:, c * FOX_HD:(c + 1) * FOX_HD] = (blk * lax.rsqrt(ms + RMS_EPS)) * gain

    @pl.when(j < nt)
    def _():
        head_norm(q_ref, qn_ref[...])

    @pl.when((j >= nt) & (j < 2 * nt))
    def _():
        head_norm(k_ref, kn_ref[...])

    @pl.when((j >= 2 * nt) & (j < 3 * nt))
    def _():
        v_ref[...] = acc

    @pl.when(j >= 3 * nt)
    def _():
        go_ref[...] = acc


def fox_in_proj(x, gain, mods, w_in, b_f, q_norm, k_norm, tm, rows_per_batch, tn=512):
    m = x.shape[0]
    n_main = 4 * FOX_DIM
    nt = FOX_DIM // tn
    w_f = w_in[:, n_main:]
    sh_spec, sc_spec = _mod_specs(mods, (0, 1), tm, rows_per_batch)
    col = lambda j, g: jnp.clip(j - g * nt, 0, nt - 1)
    flat_spec = lambda g: pl.BlockSpec((tm, tn), lambda i, j: (i, col(j, g)))
    flat = jax.ShapeDtypeStruct((m, FOX_DIM), F32)
    return pl.pallas_call(
        functools.partial(_fox_in_kernel, tn=tn),
        grid=(m // tm, n_main // tn),
        in_specs=[
            pl.BlockSpec((tm, D_MODEL), lambda i, j: (i, 0)),
            pl.BlockSpec((1, D_MODEL), lambda i, j: (0, 0)),
            sh_spec, sc_spec,
            pl.BlockSpec((D_MODEL, tn), lambda i, j: (0, j)),
            pl.BlockSpec((D_MODEL, FOX_HEADS), lambda i, j: (0, 0)),
            pl.BlockSpec((1, FOX_HEADS), lambda i, j: (0, 0)),
            pl.BlockSpec((1, FOX_HD), lambda i, j: (0, 0)),
            pl.BlockSpec((1, FOX_HD), lambda i, j: (0, 0)),
        ],
        out_specs=[flat_spec(0), flat_spec(1), flat_spec(2), flat_spec(3),
                   pl.BlockSpec((tm, FOX_HEADS), lambda i, j: (i, 0))],
        out_shape=[flat, flat, flat, flat, jax.ShapeDtypeStruct((m, FOX_HEADS), F32)],
        scratch_shapes=[pltpu.VMEM((tm, D_MODEL), BF16)],
        compiler_params=_params("arbitrary", "arbitrary"),
        name="fox_in_proj",
    )(x, gain.reshape(1, D_MODEL), mods, mods, w_in, w_f, b_f.reshape(1, FOX_HEADS),
      q_norm.reshape(1, FOX_HD), k_norm.reshape(1, FOX_HD))


def _cumsum_kernel(x_ref, o_ref):
    t = x_ref.shape[0]
    blk = LANES
    row = lax.broadcasted_iota(jnp.int32, (blk, blk), 0)
    col = lax.broadcasted_iota(jnp.int32, (blk, blk), 1)
    tri = (row >= col).astype(BF16)
    carry = jnp.zeros((1, x_ref.shape[1]), F32)
    for i in range(t // blk):
        c = _dot_exact_lhs(tri, x_ref[i * blk:(i + 1) * blk, :]) + carry
        o_ref[i * blk:(i + 1) * blk, :] = c
        carry = c[blk - 1:blk]


def cumsum_time(x):
    bsz, t, h = x.shape
    return pl.pallas_call(
        _cumsum_kernel,
        grid=(bsz,),
        in_specs=[pl.BlockSpec((None, t, h), lambda b: (b, 0, 0))],
        out_specs=pl.BlockSpec((None, t, h), lambda b: (b, 0, 0)),
        out_shape=jax.ShapeDtypeStruct((bsz, t, h), F32),
        compiler_params=_params("arbitrary"),
        name="cumsum_time",
    )(x)


def _fox_attn_kernel(q_ref, k_ref, v_ref, cc_ref, cr_ref, o_ref):
    blk = ATTN_BLOCK
    t = q_ref.shape[0]
    nt_dims = (((1,), (1,)), ((), ()))
    kb = k_ref[...].astype(BF16)
    vb = v_ref[...].astype(BF16)
    row = lax.broadcasted_iota(jnp.int32, (blk, blk), 0)
    col = lax.broadcasted_iota(jnp.int32, (blk, blk), 1)
    for i in range(t // blk):
        lo = i * blk
        q = (q_ref[lo:lo + blk, :] * (FOX_HD ** -0.5)).astype(BF16)
        cq = cc_ref[lo:lo + blk, :]
        s_d = lax.dot_general(q, kb[lo:lo + blk], nt_dims, preferred_element_type=F32)
        s_d = jnp.where(col <= row, s_d + cq - cr_ref[:, lo:lo + blk], NEG_INF)
        m = jnp.max(s_d, axis=-1, keepdims=True)
        if i > 0:
            s_p = lax.dot_general(q, kb[:lo], nt_dims, preferred_element_type=F32) + cq - cr_ref[:, :lo]
            m = jnp.maximum(m, jnp.max(s_p, axis=-1, keepdims=True))
        p_d = jnp.exp(s_d - m)
        l = jnp.sum(p_d, axis=-1, keepdims=True)
        acc = jnp.dot(p_d.astype(BF16), vb[lo:lo + blk], preferred_element_type=F32)
        if i > 0:
            p_p = jnp.exp(s_p - m)
            l = l + jnp.sum(p_p, axis=-1, keepdims=True)
            acc = acc + jnp.dot(p_p.astype(BF16), vb[:lo], preferred_element_type=F32)
        o_ref[lo:lo + blk, :] = acc / l


def fox_prompt_attn(q, k, v, cum):
    bsz, t, _ = q.shape
    cum_h = jnp.transpose(cum, (0, 2, 1))
    cum_col = cum_h.reshape(bsz, FOX_HEADS, t, 1)
    cum_row = cum_h.reshape(bsz, FOX_HEADS, 1, t)
    head_spec = pl.BlockSpec((None, t, FOX_HD), lambda b, h: (b, 0, h))
    return pl.pallas_call(
        _fox_attn_kernel,
        grid=(bsz, FOX_HEADS),
        in_specs=[
            head_spec, head_spec, head_spec,
            pl.BlockSpec((None, None, t, 1), lambda b, h: (b, h, 0, 0)),
            pl.BlockSpec((None, None, 1, t), lambda b, h: (b, h, 0, 0)),
        ],
        out_specs=head_spec,
        out_shape=jax.ShapeDtypeStruct((bsz, t, FOX_DIM), F32),
        compiler_params=_params("arbitrary", "arbitrary"),
        name="fox_prompt_attn",
    )(q, k, v, cum_col, cum_row)


def _fox_suffix_kernel(pt_ref, *rest):
    G = SUFFIX_PAGES_PER_STEP
    lf_refs = rest[:G]
    o_ref, carry_scr = rest[G:]
    page = lf_refs[0].shape[0]

    @pl.when(pl.program_id(1) == 0)
    def _():
        carry_scr[...] = jnp.zeros(carry_scr.shape, F32)

    row = lax.broadcasted_iota(jnp.int32, (page, page), 0)
    col = lax.broadcasted_iota(jnp.int32, (page, page), 1)
    upper = (col > row).astype(BF16)
    carry = carry_scr[...]
    for g in range(G):
        lf = lf_refs[g][...]
        o_ref[G - 1 - g] = _dot_exact_lhs(upper, lf) + carry
        carry = carry + jnp.sum(lf, axis=0, keepdims=True)
    carry_scr[...] = carry


def fox_suffix(cache_logf, layer, page_table):
    db, n_pages = page_table.shape
    page = cache_logf.shape[2]
    G = SUFFIX_PAGES_PER_STEP
    steps = n_pages // G
    lf_specs = [pl.BlockSpec((None, None, page, FOX_HEADS),
                             lambda b, t, pt, g=g: (layer, pt[b, n_pages - 1 - (t * G + g)], 0, 0))
                for g in range(G)]
    grid_spec = pltpu.PrefetchScalarGridSpec(
        num_scalar_prefetch=1,
        grid=(db, steps),
        in_specs=lf_specs,
        out_specs=pl.BlockSpec((None, G, page, FOX_HEADS), lambda b, t, pt: (b, steps - 1 - t, 0, 0)),
        scratch_shapes=[pltpu.VMEM((1, FOX_HEADS), F32)],
    )
    return pl.pallas_call(
        _fox_suffix_kernel,
        grid_spec=grid_spec,
        out_shape=jax.ShapeDtypeStruct((db, n_pages, page, FOX_HEADS), F32),
        compiler_params=_params("arbitrary", "arbitrary"),
        name="fox_suffix",
    )(page_table, *([cache_logf] * G))


def _fox_paged_kernel(pt_ref, q_ref, ct_ref, kn_ref, vn_ref, bn_ref, suf_ref, *rest):
    G = PAGES_PER_STEP
    k_refs = rest[:G]
    v_refs = rest[G:2 * G]
    o_ref = rest[2 * G]
    m_scr, l_scr, acc_scr = rest[2 * G + 1:]
    t = pl.program_id(1)
    page = k_refs[0].shape[0]
    rows = q_ref.shape[0]
    nt_dims = (((1,), (1,)), ((), ()))

    @pl.when(t == 0)
    def _():
        m_scr[...] = jnp.full(m_scr.shape, NEG_INF, F32)
        l_scr[...] = jnp.zeros(l_scr.shape, F32)
        acc_scr[...] = jnp.zeros(acc_scr.shape, F32)

    q = q_ref[...].astype(BF16)
    r_id = lax.broadcasted_iota(jnp.int32, (rows, LANES), 0)
    c_id = lax.broadcasted_iota(jnp.int32, (rows, LANES), 1)
    head_bias = jnp.where(r_id % FOX_HEADS == c_id % FOX_HEADS, 0.0, NEG_INF)
    head_bias = jnp.concatenate([head_bias] * (page * FOX_HEADS // LANES), axis=1) + ct_ref[...]

    def online(carry, s_list, v_list):
        m, l, acc = carry
        m_new = m
        for s in s_list:
            m_new = jnp.maximum(m_new, jnp.max(s, axis=-1, keepdims=True))
        alpha = jnp.exp(m - m_new)
        l = alpha * l
        acc = alpha * acc
        for s, v2 in zip(s_list, v_list):
            p = jnp.exp(s - m_new)
            l = l + jnp.sum(p, axis=-1, keepdims=True)
            acc = acc + jnp.dot(p.astype(BF16), v2, preferred_element_type=F32)
        return m_new, l, acc

    s_list, v_list = [], []
    for g in range(G):
        k2 = k_refs[g][...].reshape(page * FOX_HEADS, FOX_HD).astype(BF16)
        v_list.append(v_refs[g][...].reshape(page * FOX_HEADS, FOX_HD).astype(BF16))
        s_list.append(lax.dot_general(q, k2, nt_dims, preferred_element_type=F32) + (head_bias + suf_ref[g]))
    carry = online((m_scr[...], l_scr[...], acc_scr[...]), s_list, v_list)
    m_scr[...], l_scr[...], acc_scr[...] = carry

    @pl.when(t == pl.num_programs(1) - 1)
    def _():
        s = lax.dot_general(q, kn_ref[...].astype(BF16), nt_dims, preferred_element_type=F32) + bn_ref[...]
        _, l, acc = online(carry, [s], [vn_ref[...].astype(BF16)])
        o_ref[...] = acc / l


def fox_paged_attn(q, k_new, v_new, logf_new, cache_k, cache_v, cache_logf, layer, page_table):
    db, s_len, _, _ = q.shape
    n_pages = page_table.shape[1]
    page = cache_k.shape[2]
    G = PAGES_PER_STEP
    rows = s_len * FOX_HEADS
    new_cols = LANES
    assert rows % 8 == 0 and rows <= new_cols and (page * FOX_HEADS) % LANES == 0

    suf = fox_suffix(cache_logf, layer, page_table).reshape(db, n_pages, 1, page * FOX_HEADS)

    q2 = (q * (FOX_HD ** -0.5)).reshape(db, rows, FOX_HD)
    cum = jnp.cumsum(logf_new, axis=1)
    ct = cum.reshape(db, rows, 1)
    same_head = jnp.arange(FOX_HEADS)[:, None] == jnp.arange(FOX_HEADS)[None, :]
    causal = jnp.arange(s_len)[None, :] <= jnp.arange(s_len)[:, None]
    ok = causal[:, None, :, None] & same_head[None, :, None, :]
    bn = jnp.where(ok[None], cum[:, :, :, None, None] - cum[:, None, None, :, :], NEG_INF)
    bn = jnp.pad(bn.reshape(db, rows, rows), ((0, 0), (0, 0), (0, new_cols - rows)), constant_values=NEG_INF)
    kn = jnp.pad(k_new.reshape(db, rows, FOX_HD), ((0, 0), (0, new_cols - rows), (0, 0)))
    vn = jnp.pad(v_new.reshape(db, rows, FOX_HD), ((0, 0), (0, new_cols - rows), (0, 0)))

    kv_specs = [pl.BlockSpec((None, None, page, FOX_HEADS, FOX_HD),
                             lambda b, t, pt, g=g: (layer, pt[b, t * G + g], 0, 0, 0)) for g in range(G)]
    grid_spec = pltpu.PrefetchScalarGridSpec(
        num_scalar_prefetch=1,
        grid=(db, n_pages // G),
        in_specs=[
            pl.BlockSpec((None, rows, FOX_HD), lambda b, t, pt: (b, 0, 0)),
            pl.BlockSpec((None, rows, 1), lambda b, t, pt: (b, 0, 0)),
            pl.BlockSpec((None, new_cols, FOX_HD), lambda b, t, pt: (b, 0, 0)),
            pl.BlockSpec((None, new_cols, FOX_HD), lambda b, t, pt: (b, 0, 0)),
            pl.BlockSpec((None, rows, new_cols), lambda b, t, pt: (b, 0, 0)),
            pl.BlockSpec((None, G, 1, page * FOX_HEADS), lambda b, t, pt: (b, t, 0, 0)),
        ] + kv_specs + kv_specs,
        out_specs=pl.BlockSpec((None, rows, FOX_HD), lambda b, t, pt: (b, 0, 0)),
        scratch_shapes=[pltpu.VMEM((rows, 1), F32), pltpu.VMEM((rows, 1), F32),
                        pltpu.VMEM((rows, FOX_HD), F32)],
    )
    o = pl.pallas_call(
        _fox_paged_kernel,
        grid_spec=grid_spec,
        out_shape=jax.ShapeDtypeStruct((db, rows, FOX_HD), F32),
        compiler_params=_params("arbitrary", "arbitrary"),
        name="fox_paged_attn",
    )(page_table, q2, ct, kn, vn, bn, suf, *([cache_k] * G), *([cache_v] * G))
    return o.reshape(db, s_len, FOX_HEADS, FOX_HD)


def _moe_pre_kernel(x_ref, g_ref, sh_ref, sc_ref, wr_ref, br_ref, *rest):
    h_ref, lo_ref = rest[-2:]
    h = _modulate(x_ref[...], g_ref[...], sh_ref[...], sc_ref[...])
    h_ref[...] = h
    lo_ref[...] = jnp.dot(h.astype(BF16), wr_ref[...], preferred_element_type=F32) + br_ref[...]


def moe_pre(x, gain, mods, w_router, b_router, tm, rows_per_batch, n_tok, tok0, h_prev=None):
    m = x.shape[0]
    tiles = m // tm
    prev = () if h_prev is None else (h_prev,)
    extra = 1 if (h_prev is None and n_tok > m) else 0
    assert n_tok - m <= tm or not extra
    src = lambda i: jnp.minimum(i, tiles - 1)
    if mods.ndim == 3:
        mod_spec = lambda c: pl.BlockSpec((None, 1, D_MODEL),
                                          lambda i, j: ((src(i) * tm) // rows_per_batch, 0, c))
    else:
        mod_spec = lambda c: pl.BlockSpec((tm, D_MODEL), lambda i, j: (src(i), c))
    return pl.pallas_call(
        _moe_pre_kernel,
        grid=(tiles + extra, 1),
        in_specs=[
            pl.BlockSpec((tm, D_MODEL), lambda i, j: (src(i), 0)),
            pl.BlockSpec((1, D_MODEL), lambda i, j: (0, 0)),
            mod_spec(3), mod_spec(4),
            pl.BlockSpec((D_MODEL, ROUTER_COLS), lambda i, j: (0, 0)),
            pl.BlockSpec((1, ROUTER_COLS), lambda i, j: (0, 0)),
        ] + [pl.BlockSpec(memory_space=pl.ANY)] * len(prev),
        out_specs=[
            pl.BlockSpec((tm, D_MODEL), lambda i, j: (tok0 // tm + i, 0)),
            pl.BlockSpec((tm, ROUTER_COLS), lambda i, j: (src(i), 0)),
        ],
        out_shape=[jax.ShapeDtypeStruct((n_tok, D_MODEL), F32), jax.ShapeDtypeStruct((m, ROUTER_COLS), F32)],
        input_output_aliases={6: 0} if prev else {},
        compiler_params=_params("arbitrary", "arbitrary"),
        name="moe_pre",
    )(x, gain.reshape(1, D_MODEL), mods, mods, w_router, b_router, *prev)


def _moe_expert_kernel(be_ref, nu_ref, tok_ref, first_ref, next_ref, h_hbm, wg_hbm, wu_hbm, wd_hbm, y_ref,
                       x_buf, sem, w_sem, wg_f, wu_f, wd_f, wg_s, wu_s, wd_s, *, layer):
    i = pl.program_id(0)
    tb = x_buf.shape[1]
    n_used = nu_ref[0]

    def weight_copies(e):
        return (pltpu.make_async_copy(wg_hbm.at[layer, e], wg_f, w_sem.at[0]),
                pltpu.make_async_copy(wu_hbm.at[layer, e], wu_f, w_sem.at[1]),
                pltpu.make_async_copy(wd_hbm.at[layer, e], wd_f, w_sem.at[2]))

    def row_copy(blk, r, half):
        return pltpu.make_async_copy(h_hbm.at[pl.ds(tok_ref[blk * tb + r], 1)],
                                     x_buf.at[half, pl.ds(r, 1)], sem.at[half])

    def start_rows(blk):
        def body(r, c):
            row_copy(blk, r, blk % 2).start()
            return c
        lax.fori_loop(0, tb, body, 0, unroll=8)

    def wait_rows(blk):
        def body(r, c):
            row_copy(blk, r, blk % 2).wait()
            return c
        lax.fori_loop(0, tb, body, 0, unroll=8)

    @pl.when((i == 0) & (n_used > 0))
    def _():
        start_rows(0)

    @pl.when(i + 1 < n_used)
    def _():
        start_rows(i + 1)

    @pl.when((i == 0) & (n_used > 0))
    def _():
        for c in weight_copies(be_ref[0]):
            c.start()

    @pl.when(first_ref[i] == 1)
    def _():
        for c in weight_copies(be_ref[i]):
            c.wait()
        wg_s[...] = wg_f[...].astype(BF16)
        wu_s[...] = wu_f[...].astype(BF16)
        wd_s[...] = wd_f[...].astype(BF16)

        @pl.when(next_ref[i] >= 0)
        def _():
            for c in weight_copies(next_ref[i]):
                c.start()

    @pl.when(i < n_used)
    def _():
        wait_rows(i)
        x = x_buf[i % 2].astype(BF16)
        a = jnp.dot(x, wg_s[...], preferred_element_type=F32)
        u = jnp.dot(x, wu_s[...], preferred_element_type=F32)
        y_ref[...] = jnp.dot((_silu(a) * u).astype(BF16), wd_s[...], preferred_element_type=F32)

    @pl.when(i >= n_used)
    def _():
        y_ref[...] = jnp.zeros(y_ref.shape, F32)


def moe_experts(h_all, buf_tok, block_e, n_used, first, next_e, w_gate, w_up, w_down, layer):
    cap = buf_tok.shape[0]
    tb = MOE_BLOCK
    any_spec = pl.BlockSpec(memory_space=pl.ANY)
    grid_spec = pltpu.PrefetchScalarGridSpec(
        num_scalar_prefetch=5,
        grid=(cap // tb,),
        in_specs=[any_spec, any_spec, any_spec, any_spec],
        out_specs=pl.BlockSpec((tb, D_MODEL), lambda i, *_: (i, 0)),
        scratch_shapes=[pltpu.VMEM((2, tb, D_MODEL), F32), pltpu.SemaphoreType.DMA((2,)),
                        pltpu.SemaphoreType.DMA((3,)),
                        pltpu.VMEM((D_MODEL, MOE_DFF), F32), pltpu.VMEM((D_MODEL, MOE_DFF), F32),
                        pltpu.VMEM((MOE_DFF, D_MODEL), F32),
                        pltpu.VMEM((D_MODEL, MOE_DFF), BF16), pltpu.VMEM((D_MODEL, MOE_DFF), BF16),
                        pltpu.VMEM((MOE_DFF, D_MODEL), BF16)],
    )
    return pl.pallas_call(
        functools.partial(_moe_expert_kernel, layer=layer),
        grid_spec=grid_spec,
        out_shape=jax.ShapeDtypeStruct((cap, D_MODEL), F32),
        compiler_params=_params("arbitrary"),
        name="moe_experts",
    )(block_e, n_used, buf_tok, first, next_e, h_all, w_gate, w_up, w_down)


def _route_kernel(lo_ref, eid_ref, gw_ref, rank_ref, cnt_ref, carry_scr, *, n_valid):
    tm = lo_ref.shape[0]
    i = pl.program_id(0)

    @pl.when(i == 0)
    def _():
        carry_scr[...] = jnp.zeros(carry_scr.shape, F32)

    lo = lo_ref[...]
    lane_i = lax.broadcasted_iota(jnp.int32, lo.shape, 1)
    lane = lane_i.astype(F32)
    row = lax.broadcasted_iota(jnp.int32, lo.shape, 0) + i * tm

    def masked_softmax(mask):
        x = jnp.where(mask, lo, NEG_INF)
        e = jnp.where(mask, jnp.exp(x - jnp.max(x, axis=-1, keepdims=True)), 0.0)
        return e / jnp.sum(e, axis=-1, keepdims=True)

    def first_max(p, mask):
        pm = jnp.where(mask, p, -1.0)
        top = jnp.max(pm, axis=-1, keepdims=True)
        idx = jnp.min(jnp.where(pm == top, lane, float(LANES)), axis=-1, keepdims=True)
        return top, idx

    gmask = lane_i < MOE_GROUPS
    p_g, g_idx = first_max(masked_softmax(gmask), gmask)
    e_lane = lane_i - MOE_GROUPS
    emask = (e_lane >= 0) & (e_lane < MOE_EXPERTS) & ((e_lane // MOE_PER_GROUP).astype(F32) == g_idx)
    pe = masked_softmax(emask)
    w0, i0 = first_max(pe, emask)
    w1, i1 = first_max(pe, emask & (lane != i0))
    tw = w0 + w1
    two = lax.broadcasted_iota(jnp.int32, (tm, MOE_TOPK), 1)
    gw_ref[...] = jnp.where(two == 0, p_g * (w0 / tw), p_g * (w1 / tw))
    eid_ref[...] = jnp.where(two == 0, i0, i1).astype(jnp.int32) - MOE_GROUPS

    cnt = jnp.where((row < n_valid) & ((lane == i0) | (lane == i1)), 1.0, 0.0)
    r_id = lax.broadcasted_iota(jnp.int32, (tm, tm), 0)
    c_id = lax.broadcasted_iota(jnp.int32, (tm, tm), 1)
    earlier = (c_id < r_id).astype(BF16)
    before = jnp.dot(earlier, cnt.astype(BF16), preferred_element_type=F32) + carry_scr[...]
    rank0 = jnp.sum(jnp.where(lane == i0, before, 0.0), axis=-1, keepdims=True)
    rank1 = jnp.sum(jnp.where(lane == i1, before, 0.0), axis=-1, keepdims=True)
    rank_ref[...] = jnp.where(two == 0, rank0, rank1).astype(jnp.int32)
    total = carry_scr[...] + jnp.sum(cnt, axis=0, keepdims=True)
    carry_scr[...] = total
    cnt_ref[...] = total


def moe_route(logits, n_valid):
    n_pad = logits.shape[0]
    tm = ROUTE_BLOCK
    pair = lambda dt: jax.ShapeDtypeStruct((n_pad, MOE_TOPK), dt)
    pair_spec = pl.BlockSpec((tm, MOE_TOPK), lambda i: (i, 0))
    return pl.pallas_call(
        functools.partial(_route_kernel, n_valid=n_valid),
        grid=(n_pad // tm,),
        in_specs=[pl.BlockSpec((tm, ROUTER_COLS), lambda i: (i, 0))],
        out_specs=[pair_spec, pair_spec, pair_spec, pl.BlockSpec((1, ROUTER_COLS), lambda i: (0, 0))],
        out_shape=[pair(jnp.int32), pair(F32), pair(jnp.int32), jax.ShapeDtypeStruct((1, ROUTER_COLS), F32)],
        scratch_shapes=[pltpu.VMEM((1, ROUTER_COLS), F32)],
        compiler_params=_params("arbitrary"),
        name="moe_route",
    )(logits)


def _dispatch(e_ids, rank, lane_counts):
    n = e_ids.shape[0]
    tb = MOE_BLOCK
    counts = lane_counts[0, MOE_GROUPS:MOE_GROUPS + MOE_EXPERTS].astype(jnp.int32)
    padded = ((counts + tb - 1) // tb) * tb
    pend = jnp.cumsum(padded)
    pstart = pend - padded
    dest = pstart[e_ids] + rank
    n_blocks = (n * MOE_TOPK + MOE_EXPERTS * (tb - 1) + tb - 1) // tb
    tok = jnp.repeat(jnp.arange(n, dtype=jnp.int32), MOE_TOPK)
    buf_tok = jnp.zeros((n_blocks * tb,), jnp.int32).at[dest.reshape(-1)].set(tok)
    block_e = jnp.minimum(jnp.searchsorted(pend, jnp.arange(n_blocks, dtype=jnp.int32) * tb, side='right'),
                          MOE_EXPERTS - 1).astype(jnp.int32)
    n_used = (pend[-1] // tb).astype(jnp.int32)
    idx = jnp.arange(n_blocks, dtype=jnp.int32)
    first = ((idx < n_used) & ((idx == 0) | (block_e != jnp.roll(block_e, 1)))).astype(jnp.int32)
    after = (pend[block_e] // tb).astype(jnp.int32)
    next_e = jnp.where(after < n_used, block_e[jnp.minimum(after, n_blocks - 1)], -1).astype(jnp.int32)
    return dest, buf_tok, block_e, n_used.reshape(1), first, next_e


def _combine_kernel(slot_ref, x_ref, ga_ref, w_ref, y_hbm, *rest, tok0, final):
    if final:
        g_ref, o_ref, y_buf, sem = rest
    else:
        o_ref, y_buf, sem = rest
    tm = x_ref.shape[0]
    base = (tok0 + pl.program_id(0) * tm) * MOE_TOPK

    def row_copy(r, k):
        return pltpu.make_async_copy(y_hbm.at[pl.ds(slot_ref[base + r * MOE_TOPK + k], 1)],
                                     y_buf.at[k, pl.ds(r, 1)], sem.at[k])

    def start(r, c):
        for k in range(MOE_TOPK):
            row_copy(r, k).start()
        return c

    def wait(r, c):
        for k in range(MOE_TOPK):
            row_copy(r, k).wait()
        return c

    lax.fori_loop(0, tm, start, 0, unroll=8)
    lax.fori_loop(0, tm, wait, 0, unroll=8)
    w = w_ref[...]
    out = x_ref[...] + ga_ref[...] * (y_buf[0] * w[:, 0:1] + y_buf[1] * w[:, 1:2])
    if final:
        ms = jnp.mean(out * out, axis=-1, keepdims=True)
        out = (out * lax.rsqrt(ms + RMS_EPS)) * g_ref[...]
    o_ref[...] = out


def moe_combine(x, mods, y, slot_flat, gate_w, tok0, tm, rows_per_batch, final_gain=None):
    m = x.shape[0]
    final = final_gain is not None
    if mods.ndim == 3:
        ga_spec = pl.BlockSpec((None, 1, D_MODEL), lambda i, s: ((i * tm) // rows_per_batch, 0, 5))
    else:
        ga_spec = pl.BlockSpec((tm, D_MODEL), lambda i, s: (i, 5))
    row_spec = pl.BlockSpec((tm, D_MODEL), lambda i, s: (i, 0))
    in_specs = [row_spec, ga_spec,
                pl.BlockSpec((tm, MOE_TOPK), lambda i, s: (tok0 // tm + i, 0)),
                pl.BlockSpec(memory_space=pl.ANY)]
    args = [x, mods, gate_w, y]
    if final:
        in_specs.append(pl.BlockSpec((1, D_MODEL), lambda i, s: (0, 0)))
        args.append(final_gain.reshape(1, D_MODEL))
    grid_spec = pltpu.PrefetchScalarGridSpec(
        num_scalar_prefetch=1,
        grid=(m // tm,),
        in_specs=in_specs,
        out_specs=row_spec,
        scratch_shapes=[pltpu.VMEM((MOE_TOPK, tm, D_MODEL), F32), pltpu.SemaphoreType.DMA((MOE_TOPK,))],
    )
    return pl.pallas_call(
        functools.partial(_combine_kernel, tok0=tok0, final=final),
        grid_spec=grid_spec,
        out_shape=jax.ShapeDtypeStruct((m, D_MODEL), F32),
        compiler_params=_params("arbitrary"),
        name="moe_combine",
    )(slot_flat, *args)


def kernel(x_prompt, x_sample, c_prompt, c_sample, state_gla, cache_k, cache_v, cache_logf, page_table, norm_mix, norm_ffn, norm_final, w_ada, b_ada, gla_w_in, gla_w_gate_up, gla_b_gate, gla_norm, gla_w_out, fox_w_in, fox_b_f, fox_q_norm, fox_k_norm, fox_w_out, moe_w_group, moe_b_group, moe_w_expert, moe_b_expert, moe_w_gate, moe_w_up, moe_w_down):
    bsz, seq, d = x_prompt.shape
    db, ds, _ = x_sample.shape
    depth = w_ada.shape[0]
    mp, msz = bsz * seq, db * ds
    ti_p = min(1024, seq)
    tm_p = min(512, seq)
    te_p = min(256, seq)
    n_tok = mp + msz
    n_route = -(-n_tok // ROUTE_BLOCK) * ROUTE_BLOCK
    assert d == D_MODEL and seq % ti_p == 0 and seq % GLA_CHUNK == 0 and seq % ATTN_BLOCK == 0
    assert ds <= GLA_CHUNK and page_table.shape[1] % PAGES_PER_STEP == 0
    assert page_table.shape[1] % SUFFIX_PAGES_PER_STEP == 0


    c_rows = jnp.concatenate([c_prompt, c_sample], axis=0)
    c_rows = jnp.pad(c_rows, ((0, (-c_rows.shape[0]) % 8), (0, 0)))
    ada = ada_all(c_rows, w_ada, b_ada)

    xp = x_prompt.reshape(mp, d)
    xs = x_sample.reshape(msz, d)
    gla_p, gla_s, kp_l, vp_l, lfp_l, ks_l, vs_l, lfs_l = [], [], [], [], [], [], [], []
    for i in range(depth):
        j = i // 2
        mods_p = ada[i, :bsz].reshape(bsz, 1, 6 * d)
        mods_s = jnp.repeat(ada[i, bsz:bsz + db], ds, axis=0)
        if i % 2 == 0:
            zp, lgp = gla_in_proj(xp, norm_mix[i], mods_p, gla_w_in[j], gla_w_gate_up[j], gla_b_gate[j],
                                  ti_p, seq)
            s0 = jnp.zeros((bsz, GLA_HEADS, GLA_DKH, GLA_DVH), F32)
            op, s_fin = gla_scan(zp.reshape(bsz, seq, GLA_MAIN), lgp.reshape(bsz, seq, GLA_DK), s0)
            xp = out_proj(op.reshape(mp, d), zp, 2, gla_norm[j], gla_w_out[j], xp, mods_p, tm_p, seq)
            gla_p.append(s_fin)

            zs, lgs = gla_in_proj(xs, norm_mix[i], mods_s, gla_w_in[j], gla_w_gate_up[j], gla_b_gate[j],
                                  msz, msz)
            pad = ((0, 0), (0, GLA_CHUNK - ds), (0, 0))
            zs_pad = jnp.pad(zs.reshape(db, ds, GLA_MAIN), pad)
            lgs_pad = jnp.pad(lgs.reshape(db, ds, GLA_DK), pad)
            os_pad, s_new = gla_scan(zs_pad, lgs_pad, state_gla[j])
            xs = out_proj(os_pad[:, :ds].reshape(msz, d), zs, 2, gla_norm[j], gla_w_out[j], xs, mods_s,
                          msz, msz)
            gla_s.append(s_new)
        else:
            qp, kp, vp, gp, lfp = fox_in_proj(xp, norm_mix[i], mods_p, fox_w_in[j], fox_b_f[j],
                                              fox_q_norm[j], fox_k_norm[j], ti_p, seq)
            rows3 = (bsz, seq, FOX_DIM)
            lf3 = lfp.reshape(bsz, seq, FOX_HEADS)
            op = fox_prompt_attn(qp.reshape(rows3), kp.reshape(rows3), vp.reshape(rows3), cumsum_time(lf3))
            xp = out_proj(op.reshape(mp, d), gp, 0, None, fox_w_out[j], xp, mods_p, tm_p, seq)
            kp_l.append(kp.reshape(bsz, seq, FOX_HEADS, FOX_HD))
            vp_l.append(vp.reshape(bsz, seq, FOX_HEADS, FOX_HD))
            lfp_l.append(lf3)

            qs, ks, vs, gs, lfs = fox_in_proj(xs, norm_mix[i], mods_s, fox_w_in[j], fox_b_f[j],
                                              fox_q_norm[j], fox_k_norm[j], msz, msz)
            heads = (db, ds, FOX_HEADS, FOX_HD)
            lfs3 = lfs.reshape(db, ds, FOX_HEADS)
            os_ = fox_paged_attn(qs.reshape(heads), ks.reshape(heads), vs.reshape(heads), lfs3,
                                 cache_k, cache_v, cache_logf, j, page_table)
            xs = out_proj(os_.reshape(msz, d), gs, 0, None, fox_w_out[j], xs, mods_s, msz, msz)
            ks_l.append(ks.reshape(heads))
            vs_l.append(vs.reshape(heads))
            lfs_l.append(lfs3)

        w_router = jnp.pad(jnp.concatenate([moe_w_group[i], moe_w_expert[i]], axis=1),
                           ((0, 0), (0, ROUTER_COLS - MOE_GROUPS - MOE_EXPERTS))).astype(BF16)
        b_router = jnp.pad(jnp.concatenate([moe_b_group[i], moe_b_expert[i]]),
                           (0, ROUTER_COLS - MOE_GROUPS - MOE_EXPERTS)).reshape(1, ROUTER_COLS)
        h_all, lop = moe_pre(xp, norm_ffn[i], mods_p, w_router, b_router, tm_p, seq, n_tok, 0)
        h_all, los = moe_pre(xs, norm_ffn[i], mods_s, w_router, b_router, msz, msz, n_tok, mp, h_all)
        logits = jnp.concatenate([lop, los, jnp.zeros((n_route - n_tok, ROUTER_COLS), F32)], axis=0)
        e_ids, gate_w, rank, lane_counts = moe_route(logits, n_tok)
        slot, buf_tok, block_e, n_used, first, next_e = _dispatch(e_ids[:n_tok], rank[:n_tok], lane_counts)
        y = moe_experts(h_all, buf_tok, block_e, n_used, first, next_e, moe_w_gate, moe_w_up, moe_w_down, i)
        slot_flat = slot.reshape(-1)
        closing = norm_final if i == depth - 1 else None
        xp = moe_combine(xp, mods_p, y, slot_flat, gate_w, 0, te_p, seq, closing)
        xs = moe_combine(xs, mods_s, y, slot_flat, gate_w, mp, msz, msz, closing)

    y_prompt = xp.reshape(bsz, seq, d)
    y_sample = xs.reshape(db, ds, d)
    return (y_prompt, y_sample,
            jnp.stack(kp_l), jnp.stack(vp_l), jnp.stack(lfp_l), jnp.stack(gla_p),
            jnp.stack(ks_l), jnp.stack(vs_l), jnp.stack(lfs_l), jnp.stack(gla_s))
```

```python
import functools

import jax
import jax.numpy as jnp
from jax import lax
from jax.experimental import pallas as pl
from jax.experimental.pallas import tpu as pltpu

F32 = jnp.float32
BF16 = jnp.bfloat16

D_MODEL = 2048
GLA_HEADS = 4
GLA_DK = D_MODEL // 2
GLA_DV = D_MODEL
GLA_DKH = GLA_DK // GLA_HEADS
GLA_DVH = GLA_DV // GLA_HEADS
GLA_GATE_RANK = 16
GLA_GATE_NORM = 16.0
GLA_MAIN = 2 * GLA_DK + 2 * GLA_DV
FOX_HEADS = 16
FOX_HD = D_MODEL // FOX_HEADS
FOX_DIM = FOX_HEADS * FOX_HD
MOE_GROUPS = 4
MOE_PER_GROUP = 8
MOE_EXPERTS = MOE_GROUPS * MOE_PER_GROUP
MOE_TOPK = 2
MOE_DFF = D_MODEL // 4
RMS_EPS = 1e-6
NEG_INF = -1e30

VMEM_LIMIT_BYTES = 52 * 1024 * 1024
LANES = 128

GLA_CHUNK = 128
GLA_SUB = 16
ATTN_BLOCK = 256
PAGES_PER_STEP = 4
SUFFIX_PAGES_PER_STEP = 16
ROUTE_BLOCK = 256
MOE_BLOCK = 256
ROUTER_COLS = 128


def _params(*sem):
    return pltpu.CompilerParams(dimension_semantics=sem, vmem_limit_bytes=VMEM_LIMIT_BYTES)


def _log_sigmoid(x):
    return jnp.minimum(x, 0.0) - jnp.log1p(jnp.exp(-jnp.abs(x)))


def _silu(x):
    return x * jax.nn.sigmoid(x)


def _split3(a):
    hi = a.astype(BF16)
    r1 = a - hi.astype(F32)
    mid = r1.astype(BF16)
    lo = (r1 - mid.astype(F32)).astype(BF16)
    return hi, mid, lo


def _dot_exact_lhs(sel_bf16, x_f32):
    hi, mid, lo = _split3(x_f32)
    d = functools.partial(jnp.dot, preferred_element_type=F32)
    return d(sel_bf16, hi) + d(sel_bf16, mid) + d(sel_bf16, lo)


def _modulate(x, g, shift, scale):
    ms = jnp.mean(x * x, axis=-1, keepdims=True)
    return (x * lax.rsqrt(ms + RMS_EPS)) * g * (1.0 + scale) + shift


def _ada_kernel(c_ref, w_ref, b_ref, o_ref):
    a = _silu(c_ref[...]).astype(BF16)
    o_ref[...] = jnp.dot(a, w_ref[...].astype(BF16), preferred_element_type=F32) + b_ref[...]


def ada_all(c_rows, w_ada, b_ada, tn=1024):
    depth, d, n = w_ada.shape
    rows = c_rows.shape[0]
    return pl.pallas_call(
        _ada_kernel,
        grid=(depth, n // tn),
        in_specs=[
            pl.BlockSpec((rows, d), lambda l, j: (0, 0)),
            pl.BlockSpec((None, d, tn), lambda l, j: (l, 0, j)),
            pl.BlockSpec((None, 1, tn), lambda l, j: (l, 0, j)),
        ],
        out_specs=pl.BlockSpec((None, rows, tn), lambda l, j: (l, 0, j)),
        out_shape=jax.ShapeDtypeStruct((depth, rows, n), F32),
        compiler_params=_params("arbitrary", "arbitrary"),
        name="ada_all",
    )(c_rows, w_ada, b_ada.reshape(depth, 1, n))


def _mod_specs(mods, chunk_ids, tm, rows_per_batch):
    specs = []
    for c in chunk_ids:
        if mods.ndim == 3:
            specs.append(pl.BlockSpec((None, 1, D_MODEL),
                                      lambda i, j, c=c: ((i * tm) // rows_per_batch, 0, c)))
        else:
            specs.append(pl.BlockSpec((tm, D_MODEL), lambda i, j, c=c: (i, c)))
    return specs


def _gla_in_kernel(x_ref, g_ref, sh_ref, sc_ref, w_ref, wgd_ref, wup_ref, bg_ref, z_ref, lg_ref, h_scr):
    @pl.when(pl.program_id(1) == 0)
    def _():
        h = _modulate(x_ref[...], g_ref[...], sh_ref[...], sc_ref[...]).astype(BF16)
        h_scr[...] = h
        gd = jnp.dot(h, wgd_ref[...].astype(BF16), preferred_element_type=F32)
        gate = jnp.dot(gd.astype(BF16), wup_ref[...].astype(BF16), preferred_element_type=F32) + bg_ref[...]
        lg_ref[...] = _log_sigmoid(gate) * (1.0 / GLA_GATE_NORM)

    z_ref[...] = jnp.dot(h_scr[...], w_ref[...].astype(BF16), preferred_element_type=F32)


def gla_in_proj(x, gain, mods, w_in_all, layer, w_gate_up, b_gate, tm, rows_per_batch, tn=512):
    m = x.shape[0]
    w_gd = w_in_all[layer, :, GLA_MAIN:]
    sh_spec, sc_spec = _mod_specs(mods, (0, 1), tm, rows_per_batch)
    return pl.pallas_call(
        _gla_in_kernel,
        grid=(m // tm, GLA_MAIN // tn),
        in_specs=[
            pl.BlockSpec((tm, D_MODEL), lambda i, j: (i, 0)),
            pl.BlockSpec((1, D_MODEL), lambda i, j: (0, 0)),
            sh_spec, sc_spec,
            pl.BlockSpec((None, D_MODEL, tn), lambda i, j: (layer, 0, j)),
            pl.BlockSpec((D_MODEL, GLA_GATE_RANK), lambda i, j: (0, 0)),
            pl.BlockSpec((GLA_GATE_RANK, GLA_DK), lambda i, j: (0, 0)),
            pl.BlockSpec((1, GLA_DK), lambda i, j: (0, 0)),
        ],
        out_specs=[
            pl.BlockSpec((tm, tn), lambda i, j: (i, j)),
            pl.BlockSpec((tm, GLA_DK), lambda i, j: (i, 0)),
        ],
        out_shape=[jax.ShapeDtypeStruct((m, GLA_MAIN), F32), jax.ShapeDtypeStruct((m, GLA_DK), F32)],
        scratch_shapes=[pltpu.VMEM((tm, D_MODEL), BF16)],
        compiler_params=_params("arbitrary", "arbitrary"),
        name="gla_in_proj",
    )(x, gain.reshape(1, D_MODEL), mods, mods, w_in_all, w_gd, w_gate_up, b_gate.reshape(1, GLA_DK))


def _gla_scan_kernel(q_ref, k_ref, v_ref, lg_ref, s0_ref, o_ref, s_ref):
    C, R = GLA_CHUNK, GLA_SUB

    @pl.when(pl.program_id(2) == 0)
    def _():
        s_ref[...] = s0_ref[...]

    q = q_ref[...] * (GLA_DKH ** -0.5)
    k = k_ref[...]
    v = v_ref[...]
    vb = v.astype(BF16)
    row = lax.broadcasted_iota(jnp.int32, (C, C), 0)
    col = lax.broadcasted_iota(jnp.int32, (C, C), 1)
    tri = (row >= col).astype(BF16)
    b = _dot_exact_lhs(tri, lg_ref[...])
    state = s_ref[...]
    inter = jnp.dot((q * jnp.exp(b)).astype(BF16), state.astype(BF16), preferred_element_type=F32)

    t_idx = lax.broadcasted_iota(jnp.int32, (R, 1), 0)
    for i in range(C // R):
        lo = i * R
        bi = b[lo:lo + R]
        qi = q[lo:lo + R]
        ki = k[lo:lo + R]
        vi = v[lo:lo + R]
        oi = inter[lo:lo + R]
        if i > 0:
            b_ref_row = b[lo - 1:lo]
            qe = (qi * jnp.exp(bi - b_ref_row)).astype(BF16)
            ke = (k[:lo] * jnp.exp(b_ref_row - b[:lo])).astype(BF16)
            a = lax.dot_general(qe, ke, (((1,), (1,)), ((), ())), preferred_element_type=F32)
            oi = oi + jnp.dot(a.astype(BF16), vb[:lo], preferred_element_type=F32)
        for s in range(R):
            rel = jnp.where(t_idx >= s, bi - bi[s:s + 1], NEG_INF)
            w = jnp.sum(qi * ki[s:s + 1] * jnp.exp(rel), axis=-1, keepdims=True)
            oi = oi + w * vi[s:s + 1]
        o_ref[lo:lo + R, :] = oi

    b_t = b.T
    b_last = b_t[:, C - 1:C]
    ke_t = (k.T * jnp.exp(b_last - b_t)).astype(BF16)
    s_ref[...] = jnp.exp(b_last) * state + jnp.dot(ke_t, vb, preferred_element_type=F32)


def gla_scan(z, lg, s0):
    bsz, t, _ = z.shape
    C = GLA_CHUNK
    kq = GLA_DK // GLA_DKH
    return pl.pallas_call(
        _gla_scan_kernel,
        grid=(bsz, GLA_HEADS, t // C),
        in_specs=[
            pl.BlockSpec((None, C, GLA_DKH), lambda b, h, c: (b, c, h)),
            pl.BlockSpec((None, C, GLA_DKH), lambda b, h, c: (b, c, kq + h)),
            pl.BlockSpec((None, C, GLA_DVH), lambda b, h, c: (b, c, (2 * GLA_DK) // GLA_DVH + h)),
            pl.BlockSpec((None, C, GLA_DKH), lambda b, h, c: (b, c, h)),
            pl.BlockSpec((None, None, GLA_DKH, GLA_DVH), lambda b, h, c: (b, h, 0, 0)),
        ],
        out_specs=[
            pl.BlockSpec((None, C, GLA_DVH), lambda b, h, c: (b, c, h)),
            pl.BlockSpec((None, None, GLA_DKH, GLA_DVH), lambda b, h, c: (b, h, 0, 0)),
        ],
        out_shape=[jax.ShapeDtypeStruct((bsz, t, GLA_DV), F32),
                   jax.ShapeDtypeStruct((bsz, GLA_HEADS, GLA_DKH, GLA_DVH), F32)],
        compiler_params=_params("arbitrary", "arbitrary", "arbitrary"),
        name="gla_scan",
    )(z, z, z, lg, s0)


def _gla_out_kernel(o_ref, g_ref, ng_ref, w_ref, x_ref, ga_ref, y_ref, p_scr):
    @pl.when(pl.program_id(1) == 0)
    def _():
        gate = _silu(g_ref[...])
        for h in range(GLA_HEADS):
            sl = slice(h * GLA_DVH, (h + 1) * GLA_DVH)
            o = o_ref[:, sl]
            ms = jnp.mean(o * o, axis=-1, keepdims=True)
            p_scr[:, sl] = ((o * lax.rsqrt(ms + RMS_EPS)) * ng_ref[...] * gate[:, sl]).astype(BF16)

    y_ref[...] = x_ref[...] + ga_ref[...] * jnp.dot(p_scr[...], w_ref[...].astype(BF16),
                                                    preferred_element_type=F32)


def _fox_out_kernel(o_ref, g_ref, w_ref, x_ref, ga_ref, y_ref, p_scr):
    @pl.when(pl.program_id(1) == 0)
    def _():
        p_scr[...] = (o_ref[...] * jax.nn.sigmoid(g_ref[...])).astype(BF16)

    y_ref[...] = x_ref[...] + ga_ref[...] * jnp.dot(p_scr[...], w_ref[...].astype(BF16),
                                                    preferred_element_type=F32)


def out_proj(o, z, g_block, norm_gain, w_out_all, layer, x, mods, tm, rows_per_batch, tn=512):
    m = x.shape[0]
    row_spec = pl.BlockSpec((tm, D_MODEL), lambda i, j: (i, 0))
    in_specs = [row_spec, pl.BlockSpec((tm, D_MODEL), lambda i, j: (i, g_block))]
    args = [o, z]
    if norm_gain is not None:
        in_specs.append(pl.BlockSpec((1, GLA_DVH), lambda i, j: (0, 0)))
        args.append(norm_gain.reshape(1, GLA_DVH))
        body = _gla_out_kernel
    else:
        body = _fox_out_kernel
    in_specs += [pl.BlockSpec((None, D_MODEL, tn), lambda i, j: (layer, 0, j)),
                 pl.BlockSpec((tm, tn), lambda i, j: (i, j))]
    args += [w_out_all, x]
    if mods.ndim == 3:
        ga_spec = pl.BlockSpec((None, 1, tn),
                               lambda i, j: ((i * tm) // rows_per_batch, 0, 2 * (D_MODEL // tn) + j))
    else:
        ga_spec = pl.BlockSpec((tm, tn), lambda i, j: (i, 2 * (D_MODEL // tn) + j))
    in_specs.append(ga_spec)
    args.append(mods)
    return pl.pallas_call(
        body,
        grid=(m // tm, D_MODEL // tn),
        in_specs=in_specs,
        out_specs=pl.BlockSpec((tm, tn), lambda i, j: (i, j)),
        out_shape=jax.ShapeDtypeStruct((m, D_MODEL), F32),
        scratch_shapes=[pltpu.VMEM((tm, D_MODEL), BF16)],
        compiler_params=_params("arbitrary", "arbitrary"),
        name="out_proj",
    )(*args)


def _fox_in_kernel(x_ref, g_ref, sh_ref, sc_ref, w_ref, wf_ref, bf_ref, qn_ref, kn_ref,
                   q_ref, k_ref, v_ref, go_ref, lf_ref, h_scr, *, tn):
    j = pl.program_id(1)
    nt = FOX_DIM // tn

    @pl.when(j == 0)
    def _():
        h = _modulate(x_ref[...], g_ref[...], sh_ref[...], sc_ref[...]).astype(BF16)
        h_scr[...] = h
        f = jnp.dot(h, wf_ref[...].astype(BF16), preferred_element_type=F32)
        lf_ref[...] = _log_sigmoid(f + bf_ref[...])

    acc = jnp.dot(h_scr[...], w_ref[...].astype(BF16), preferred_element_type=F32)

    def head_norm(o_ref, gain):
        for c in range(tn // FOX_HD):
            blk = acc[:, c * FOX_HD:(c + 1) * FOX_HD]
            ms = jnp.mean(blk * blk, axis=-1, keepdims=True)
            o_ref[:, c * FOX_HD:(c + 1) * FOX_HD] = (blk * lax.rsqrt(ms + RMS_EPS)) * gain

    @pl.when(j < nt)
    def _():
        head_norm(q_ref, qn_ref[...])

    @pl.when((j >= nt) & (j < 2 * nt))
    def _():
        head_norm(k_ref, kn_ref[...])

    @pl.when((j >= 2 * nt) & (j < 3 * nt))
    def _():
        v_ref[...] = acc

    @pl.when(j >= 3 * nt)
    def _():
        go_ref[...] = acc


def fox_in_proj(x, gain, mods, w_in_all, layer, b_f, q_norm, k_norm, tm, rows_per_batch, tn=512):
    m = x.shape[0]
    n_main = 4 * FOX_DIM
    nt = FOX_DIM // tn
    w_f = w_in_all[layer, :, n_main:]
    sh_spec, sc_spec = _mod_specs(mods, (0, 1), tm, rows_per_batch)
    col = lambda j, g: jnp.clip(j - g * nt, 0, nt - 1)
    flat_spec = lambda g: pl.BlockSpec((tm, tn), lambda i, j: (i, col(j, g)))
    flat = jax.ShapeDtypeStruct((m, FOX_DIM), F32)
    return pl.pallas_call(
        functools.partial(_fox_in_kernel, tn=tn),
        grid=(m // tm, n_main // tn),
        in_specs=[
            pl.BlockSpec((tm, D_MODEL), lambda i, j: (i, 0)),
            pl.BlockSpec((1, D_MODEL), lambda i, j: (0, 0)),
            sh_spec, sc_spec,
            pl.BlockSpec((None, D_MODEL, tn), lambda i, j: (layer, 0, j)),
            pl.BlockSpec((D_MODEL, FOX_HEADS), lambda i, j: (0, 0)),
            pl.BlockSpec((1, FOX_HEADS), lambda i, j: (0, 0)),
            pl.BlockSpec((1, FOX_HD), lambda i, j: (0, 0)),
            pl.BlockSpec((1, FOX_HD), lambda i, j: (0, 0)),
        ],
        out_specs=[flat_spec(0), flat_spec(1), flat_spec(2), flat_spec(3),
                   pl.BlockSpec((tm, FOX_HEADS), lambda i, j: (i, 0))],
        out_shape=[flat, flat, flat, flat, jax.ShapeDtypeStruct((m, FOX_HEADS), F32)],
        scratch_shapes=[pltpu.VMEM((tm, D_MODEL), BF16)],
        compiler_params=_params("arbitrary", "arbitrary"),
        name="fox_in_proj",
    )(x, gain.reshape(1, D_MODEL), mods, mods, w_in_all, w_f, b_f.reshape(1, FOX_HEADS),
      q_norm.reshape(1, FOX_HD), k_norm.reshape(1, FOX_HD))


def _cumsum_kernel(x_ref, o_ref):
    t = x_ref.shape[0]
    blk = LANES
    row = lax.broadcasted_iota(jnp.int32, (blk, blk), 0)
    col = lax.broadcasted_iota(jnp.int32, (blk, blk), 1)
    tri = (row >= col).astype(BF16)
    carry = jnp.zeros((1, x_ref.shape[1]), F32)
    for i in range(t // blk):
        c = _dot_exact_lhs(tri, x_ref[i * blk:(i + 1) * blk, :]) + carry
        o_ref[i * blk:(i + 1) * blk, :] = c
        carry = c[blk - 1:blk]


def cumsum_time(x):
    bsz, t, h = x.shape
    return pl.pallas_call(
        _cumsum_kernel,
        grid=(bsz,),
        in_specs=[pl.BlockSpec((None, t, h), lambda b: (b, 0, 0))],
        out_specs=pl.BlockSpec((None, t, h), lambda b: (b, 0, 0)),
        out_shape=jax.ShapeDtypeStruct((bsz, t, h), F32),
        compiler_params=_params("arbitrary"),
        name="cumsum_time",
    )(x)


def _fox_attn_kernel(q_ref, k_ref, v_ref, cc_ref, cr_ref, o_ref):
    blk = ATTN_BLOCK
    t = q_ref.shape[0]
    nt_dims = (((1,), (1,)), ((), ()))
    kb = k_ref[...].astype(BF16)
    vb = v_ref[...].astype(BF16)
    row = lax.broadcasted_iota(jnp.int32, (blk, blk), 0)
    col = lax.broadcasted_iota(jnp.int32, (blk, blk), 1)
    for i in range(t // blk):
        lo = i * blk
        q = (q_ref[lo:lo + blk, :] * (FOX_HD ** -0.5)).astype(BF16)
        cq = cc_ref[lo:lo + blk, :]
        s_d = lax.dot_general(q, kb[lo:lo + blk], nt_dims, preferred_element_type=F32)
        s_d = jnp.where(col <= row, s_d + cq - cr_ref[:, lo:lo + blk], NEG_INF)
        m = jnp.max(s_d, axis=-1, keepdims=True)
        if i > 0:
            s_p = lax.dot_general(q, kb[:lo], nt_dims, preferred_element_type=F32) + cq - cr_ref[:, :lo]
            m = jnp.maximum(m, jnp.max(s_p, axis=-1, keepdims=True))
        p_d = jnp.exp(s_d - m)
        l = jnp.sum(p_d, axis=-1, keepdims=True)
        acc = jnp.dot(p_d.astype(BF16), vb[lo:lo + blk], preferred_element_type=F32)
        if i > 0:
            p_p = jnp.exp(s_p - m)
            l = l + jnp.sum(p_p, axis=-1, keepdims=True)
            acc = acc + jnp.dot(p_p.astype(BF16), vb[:lo], preferred_element_type=F32)
        o_ref[lo:lo + blk, :] = acc / l


def fox_prompt_attn(q, k, v, cum):
    bsz, t, _ = q.shape
    cum_h = jnp.transpose(cum, (0, 2, 1))
    cum_col = cum_h.reshape(bsz, FOX_HEADS, t, 1)
    cum_row = cum_h.reshape(bsz, FOX_HEADS, 1, t)
    head_spec = pl.BlockSpec((None, t, FOX_HD), lambda b, h: (b, 0, h))
    return pl.pallas_call(
        _fox_attn_kernel,
        grid=(bsz, FOX_HEADS),
        in_specs=[
            head_spec, head_spec, head_spec,
            pl.BlockSpec((None, None, t, 1), lambda b, h: (b, h, 0, 0)),
            pl.BlockSpec((None, None, 1, t), lambda b, h: (b, h, 0, 0)),
        ],
        out_specs=head_spec,
        out_shape=jax.ShapeDtypeStruct((bsz, t, FOX_DIM), F32),
        compiler_params=_params("arbitrary", "arbitrary"),
        name="fox_prompt_attn",
    )(q, k, v, cum_col, cum_row)


def _fox_suffix_kernel(pt_ref, *rest):
    G = SUFFIX_PAGES_PER_STEP
    lf_refs = rest[:G]
    o_ref, carry_scr = rest[G:]
    width = lf_refs[0].shape[1]

    @pl.when(pl.program_id(1) == 0)
    def _():
        carry_scr[...] = jnp.zeros(carry_scr.shape, F32)

    lf = jnp.concatenate([r[...] for r in lf_refs], axis=0)
    lane = lax.broadcasted_iota(jnp.int32, (G, width), 1)
    inc = lf
    s = FOX_HEADS
    while s < width:
        inc = inc + jnp.where(lane + s < width, pltpu.roll(inc, width - s, axis=1), 0.0)
        s *= 2
    tot = jnp.where(lane < FOX_HEADS, inc, 0.0)
    s = FOX_HEADS
    while s < width:
        tot = tot + pltpu.roll(tot, s, axis=1)
        s *= 2
    later = inc - lf
    carry = carry_scr[...]
    for g in range(G):
        o_ref[G - 1 - g] = later[g:g + 1] + carry
        carry = carry + tot[g:g + 1]
    carry_scr[...] = carry


def fox_suffix(cache_logf, layer, page_table):
    db, n_pages = page_table.shape
    n_fox, n_pool, page, _ = cache_logf.shape
    width = page * FOX_HEADS
    G = SUFFIX_PAGES_PER_STEP
    steps = n_pages // G
    lf_flat = cache_logf.reshape(n_fox, n_pool, 1, width)
    lf_specs = [pl.BlockSpec((None, None, 1, width),
                             lambda b, t, pt, g=g: (layer, pt[b, n_pages - 1 - (t * G + g)], 0, 0))
                for g in range(G)]
    grid_spec = pltpu.PrefetchScalarGridSpec(
        num_scalar_prefetch=1,
        grid=(db, steps),
        in_specs=lf_specs,
        out_specs=pl.BlockSpec((None, G, 1, width), lambda b, t, pt: (b, steps - 1 - t, 0, 0)),
        scratch_shapes=[pltpu.VMEM((1, width), F32)],
    )
    return pl.pallas_call(
        _fox_suffix_kernel,
        grid_spec=grid_spec,
        out_shape=jax.ShapeDtypeStruct((db, n_pages, 1, width), F32),
        compiler_params=_params("arbitrary", "arbitrary"),
        name="fox_suffix",
    )(page_table, *([lf_flat] * G))


def _fox_paged_kernel(pt_ref, q_ref, ct_ref, kn_ref, vn_ref, bn_ref, suf_ref, *rest):
    G = PAGES_PER_STEP
    k_refs = rest[:G]
    v_refs = rest[G:2 * G]
    o_ref = rest[2 * G]
    m_scr, l_scr, acc_scr = rest[2 * G + 1:]
    t = pl.program_id(1)
    page = k_refs[0].shape[0]
    rows = q_ref.shape[0]
    nt_dims = (((1,), (1,)), ((), ()))

    @pl.when(t == 0)
    def _():
        m_scr[...] = jnp.full(m_scr.shape, NEG_INF, F32)
        l_scr[...] = jnp.zeros(l_scr.shape, F32)
        acc_scr[...] = jnp.zeros(acc_scr.shape, F32)

    q = q_ref[...].astype(BF16)
    r_id = lax.broadcasted_iota(jnp.int32, (rows, LANES), 0)
    c_id = lax.broadcasted_iota(jnp.int32, (rows, LANES), 1)
    head_bias = jnp.where(r_id % FOX_HEADS == c_id % FOX_HEADS, 0.0, NEG_INF)
    head_bias = jnp.concatenate([head_bias] * (page * FOX_HEADS // LANES), axis=1) + ct_ref[...]

    def online(carry, s_list, v_list):
        m, l, acc = carry
        m_new = m
        for s in s_list:
            m_new = jnp.maximum(m_new, jnp.max(s, axis=-1, keepdims=True))
        alpha = jnp.exp(m - m_new)
        l = alpha * l
        acc = alpha * acc
        for s, v2 in zip(s_list, v_list):
            p = jnp.exp(s - m_new)
            l = l + jnp.sum(p, axis=-1, keepdims=True)
            acc = acc + jnp.dot(p.astype(BF16), v2, preferred_element_type=F32)
        return m_new, l, acc

    s_list, v_list = [], []
    for g in range(G):
        k2 = k_refs[g][...].reshape(page * FOX_HEADS, FOX_HD).astype(BF16)
        v_list.append(v_refs[g][...].reshape(page * FOX_HEADS, FOX_HD).astype(BF16))
        s_list.append(lax.dot_general(q, k2, nt_dims, preferred_element_type=F32) + (head_bias + suf_ref[g]))
    carry = online((m_scr[...], l_scr[...], acc_scr[...]), s_list, v_list)
    m_scr[...], l_scr[...], acc_scr[...] = carry

    @pl.when(t == pl.num_programs(1) - 1)
    def _():
        s = lax.dot_general(q, kn_ref[...].astype(BF16), nt_dims, preferred_element_type=F32) + bn_ref[...]
        _, l, acc = online(carry, [s], [vn_ref[...].astype(BF16)])
        o_ref[...] = acc / l


def fox_paged_attn(q, k_new, v_new, logf_new, cache_k, cache_v, cache_logf, layer, page_table):
    db, s_len, _, _ = q.shape
    n_pages = page_table.shape[1]
    page = cache_k.shape[2]
    G = PAGES_PER_STEP
    rows = s_len * FOX_HEADS
    new_cols = LANES
    assert rows % 8 == 0 and rows <= new_cols and (page * FOX_HEADS) % LANES == 0

    suf = fox_suffix(cache_logf, layer, page_table)

    q2 = (q * (FOX_HD ** -0.5)).reshape(db, rows, FOX_HD)
    cum = jnp.cumsum(logf_new, axis=1)
    ct = cum.reshape(db, rows, 1)
    same_head = jnp.arange(FOX_HEADS)[:, None] == jnp.arange(FOX_HEADS)[None, :]
    causal = jnp.arange(s_len)[None, :] <= jnp.arange(s_len)[:, None]
    ok = causal[:, None, :, None] & same_head[None, :, None, :]
    bn = jnp.where(ok[None], cum[:, :, :, None, None] - cum[:, None, None, :, :], NEG_INF)
    bn = jnp.pad(bn.reshape(db, rows, rows), ((0, 0), (0, 0), (0, new_cols - rows)), constant_values=NEG_INF)
    kn = jnp.pad(k_new.reshape(db, rows, FOX_HD), ((0, 0), (0, new_cols - rows), (0, 0)))
    vn = jnp.pad(v_new.reshape(db, rows, FOX_HD), ((0, 0), (0, new_cols - rows), (0, 0)))

    kv_specs = [pl.BlockSpec((None, None, page, FOX_HEADS, FOX_HD),
                             lambda b, t, pt, g=g: (layer, pt[b, t * G + g], 0, 0, 0)) for g in range(G)]
    grid_spec = pltpu.PrefetchScalarGridSpec(
        num_scalar_prefetch=1,
        grid=(db, n_pages // G),
        in_specs=[
            pl.BlockSpec((None, rows, FOX_HD), lambda b, t, pt: (b, 0, 0)),
            pl.BlockSpec((None, rows, 1), lambda b, t, pt: (b, 0, 0)),
            pl.BlockSpec((None, new_cols, FOX_HD), lambda b, t, pt: (b, 0, 0)),
            pl.BlockSpec((None, new_cols, FOX_HD), lambda b, t, pt: (b, 0, 0)),
            pl.BlockSpec((None, rows, new_cols), lambda b, t, pt: (b, 0, 0)),
            pl.BlockSpec((None, G, 1, page * FOX_HEADS), lambda b, t, pt: (b, t, 0, 0)),
        ] + kv_specs + kv_specs,
        out_specs=pl.BlockSpec((None, rows, FOX_HD), lambda b, t, pt: (b, 0, 0)),
        scratch_shapes=[pltpu.VMEM((rows, 1), F32), pltpu.VMEM((rows, 1), F32),
                        pltpu.VMEM((rows, FOX_HD), F32)],
    )
    o = pl.pallas_call(
        _fox_paged_kernel,
        grid_spec=grid_spec,
        out_shape=jax.ShapeDtypeStruct((db, rows, FOX_HD), F32),
        compiler_params=_params("arbitrary", "arbitrary"),
        name="fox_paged_attn",
    )(page_table, q2, ct, kn, vn, bn, suf, *([cache_k] * G), *([cache_v] * G))
    return o.reshape(db, s_len, FOX_HEADS, FOX_HD)


def _moe_pre_kernel(x_ref, g_ref, sh_ref, sc_ref, wr_ref, br_ref, *rest):
    h_ref, lo_ref = rest[-2:]
    h = _modulate(x_ref[...], g_ref[...], sh_ref[...], sc_ref[...])
    h_ref[...] = h
    lo_ref[...] = jnp.dot(h.astype(BF16), wr_ref[...], preferred_element_type=F32) + br_ref[...]


def moe_pre(x, gain, mods, w_router, b_router, tm, rows_per_batch, n_tok, tok0, h_prev=None):
    m = x.shape[0]
    tiles = m // tm
    prev = () if h_prev is None else (h_prev,)
    extra = 1 if (h_prev is None and n_tok > m) else 0
    assert n_tok - m <= tm or not extra
    src = lambda i: jnp.minimum(i, tiles - 1)
    if mods.ndim == 3:
        mod_spec = lambda c: pl.BlockSpec((None, 1, D_MODEL),
                                          lambda i, j: ((src(i) * tm) // rows_per_batch, 0, c))
    else:
        mod_spec = lambda c: pl.BlockSpec((tm, D_MODEL), lambda i, j: (src(i), c))
    return pl.pallas_call(
        _moe_pre_kernel,
        grid=(tiles + extra, 1),
        in_specs=[
            pl.BlockSpec((tm, D_MODEL), lambda i, j: (src(i), 0)),
            pl.BlockSpec((1, D_MODEL), lambda i, j: (0, 0)),
            mod_spec(3), mod_spec(4),
            pl.BlockSpec((D_MODEL, ROUTER_COLS), lambda i, j: (0, 0)),
            pl.BlockSpec((1, ROUTER_COLS), lambda i, j: (0, 0)),
        ] + [pl.BlockSpec(memory_space=pl.ANY)] * len(prev),
        out_specs=[
            pl.BlockSpec((tm, D_MODEL), lambda i, j: (tok0 // tm + i, 0)),
            pl.BlockSpec((tm, ROUTER_COLS), lambda i, j: (src(i), 0)),
        ],
        out_shape=[jax.ShapeDtypeStruct((n_tok, D_MODEL), F32), jax.ShapeDtypeStruct((m, ROUTER_COLS), F32)],
        input_output_aliases={6: 0} if prev else {},
        compiler_params=_params("arbitrary", "arbitrary"),
        name="moe_pre",
    )(x, gain.reshape(1, D_MODEL), mods, mods, w_router, b_router, *prev)


def _moe_expert_kernel(be_ref, nu_ref, tok_ref, first_ref, next_ref, h_hbm, wg_hbm, wu_hbm, wd_hbm, y_ref,
                       x_buf, sem, w_sem, wg_f, wu_f, wd_f, wg_s, wu_s, wd_s, *, layer):
    i = pl.program_id(0)
    tb = x_buf.shape[1]
    n_used = nu_ref[0]

    def weight_copies(e):
        return (pltpu.make_async_copy(wg_hbm.at[layer, e], wg_f, w_sem.at[0]),
                pltpu.make_async_copy(wu_hbm.at[layer, e], wu_f, w_sem.at[1]),
                pltpu.make_async_copy(wd_hbm.at[layer, e], wd_f, w_sem.at[2]))

    def row_copy(blk, r, half):
        return pltpu.make_async_copy(h_hbm.at[pl.ds(tok_ref[blk * tb + r], 1)],
                                     x_buf.at[half, pl.ds(r, 1)], sem.at[half])

    def start_rows(blk):
        def body(r, c):
            row_copy(blk, r, blk % 2).start()
            return c
        lax.fori_loop(0, tb, body, 0, unroll=8)

    def wait_rows(blk):
        def body(r, c):
            row_copy(blk, r, blk % 2).wait()
            return c
        lax.fori_loop(0, tb, body, 0, unroll=8)

    @pl.when((i == 0) & (n_used > 0))
    def _():
        start_rows(0)

    @pl.when(i + 1 < n_used)
    def _():
        start_rows(i + 1)

    @pl.when((i == 0) & (n_used > 0))
    def _():
        for c in weight_copies(be_ref[0]):
            c.start(priority=1)

    @pl.when(first_ref[i] == 1)
    def _():
        for c in weight_copies(be_ref[i]):
            c.wait()
        wg_s[...] = wg_f[...].astype(BF16)
        wu_s[...] = wu_f[...].astype(BF16)
        wd_s[...] = wd_f[...].astype(BF16)

        @pl.when(next_ref[i] >= 0)
        def _():
            for c in weight_copies(next_ref[i]):
                c.start(priority=1)

    @pl.when(i < n_used)
    def _():
        wait_rows(i)
        x = x_buf[i % 2].astype(BF16)
        a = jnp.dot(x, wg_s[...], preferred_element_type=F32)
        u = jnp.dot(x, wu_s[...], preferred_element_type=F32)
        y_ref[...] = jnp.dot((_silu(a) * u).astype(BF16), wd_s[...], preferred_element_type=F32)

    @pl.when(i >= n_used)
    def _():
        y_ref[...] = jnp.zeros(y_ref.shape, F32)


def moe_experts(h_all, buf_tok, block_e, n_used, first, next_e, w_gate, w_up, w_down, layer):
    cap = buf_tok.shape[0]
    tb = MOE_BLOCK
    any_spec = pl.BlockSpec(memory_space=pl.ANY)
    grid_spec = pltpu.PrefetchScalarGridSpec(
        num_scalar_prefetch=5,
        grid=(cap // tb,),
        in_specs=[any_spec, any_spec, any_spec, any_spec],
        out_specs=pl.BlockSpec((tb, D_MODEL), lambda i, *_: (i, 0)),
        scratch_shapes=[pltpu.VMEM((2, tb, D_MODEL), F32), pltpu.SemaphoreType.DMA((2,)),
                        pltpu.SemaphoreType.DMA((3,)),
                        pltpu.VMEM((D_MODEL, MOE_DFF), F32), pltpu.VMEM((D_MODEL, MOE_DFF), F32),
                        pltpu.VMEM((MOE_DFF, D_MODEL), F32),
                        pltpu.VMEM((D_MODEL, MOE_DFF), BF16), pltpu.VMEM((D_MODEL, MOE_DFF), BF16),
                        pltpu.VMEM((MOE_DFF, D_MODEL), BF16)],
    )
    return pl.pallas_call(
        functools.partial(_moe_expert_kernel, layer=layer),
        grid_spec=grid_spec,
        out_shape=jax.ShapeDtypeStruct((cap, D_MODEL), F32),
        compiler_params=_params("arbitrary"),
        name="moe_experts",
    )(block_e, n_used, buf_tok, first, next_e, h_all, w_gate, w_up, w_down)


def _route_kernel(lo_ref, eid_ref, gw_ref, rank_ref, cnt_ref, carry_scr, *, n_valid):
    tm = lo_ref.shape[0]
    i = pl.program_id(0)

    @pl.when(i == 0)
    def _():
        carry_scr[...] = jnp.zeros(carry_scr.shape, F32)

    lo = lo_ref[...]
    lane_i = lax.broadcasted_iota(jnp.int32, lo.shape, 1)
    lane = lane_i.astype(F32)
    row = lax.broadcasted_iota(jnp.int32, lo.shape, 0) + i * tm

    def masked_softmax(mask):
        x = jnp.where(mask, lo, NEG_INF)
        e = jnp.where(mask, jnp.exp(x - jnp.max(x, axis=-1, keepdims=True)), 0.0)
        return e / jnp.sum(e, axis=-1, keepdims=True)

    def first_max(p, mask):
        pm = jnp.where(mask, p, -1.0)
        top = jnp.max(pm, axis=-1, keepdims=True)
        idx = jnp.min(jnp.where(pm == top, lane, float(LANES)), axis=-1, keepdims=True)
        return top, idx

    gmask = lane_i < MOE_GROUPS
    p_g, g_idx = first_max(masked_softmax(gmask), gmask)
    e_lane = lane_i - MOE_GROUPS
    emask = (e_lane >= 0) & (e_lane < MOE_EXPERTS) & ((e_lane // MOE_PER_GROUP).astype(F32) == g_idx)
    pe = masked_softmax(emask)
    w0, i0 = first_max(pe, emask)
    w1, i1 = first_max(pe, emask & (lane != i0))
    tw = w0 + w1
    two = lax.broadcasted_iota(jnp.int32, (tm, MOE_TOPK), 1)
    gw_ref[...] = jnp.where(two == 0, p_g * (w0 / tw), p_g * (w1 / tw))
    eid_ref[...] = jnp.where(two == 0, i0, i1).astype(jnp.int32) - MOE_GROUPS

    cnt = jnp.where((row < n_valid) & ((lane == i0) | (lane == i1)), 1.0, 0.0)
    r_id = lax.broadcasted_iota(jnp.int32, (tm, tm), 0)
    c_id = lax.broadcasted_iota(jnp.int32, (tm, tm), 1)
    earlier = (c_id < r_id).astype(BF16)
    before = jnp.dot(earlier, cnt.astype(BF16), preferred_element_type=F32) + carry_scr[...]
    rank0 = jnp.sum(jnp.where(lane == i0, before, 0.0), axis=-1, keepdims=True)
    rank1 = jnp.sum(jnp.where(lane == i1, before, 0.0), axis=-1, keepdims=True)
    rank_ref[...] = jnp.where(two == 0, rank0, rank1).astype(jnp.int32)
    total = carry_scr[...] + jnp.sum(cnt, axis=0, keepdims=True)
    carry_scr[...] = total
    cnt_ref[...] = total


def moe_route(logits, n_valid):
    n_pad = logits.shape[0]
    tm = ROUTE_BLOCK
    pair = lambda dt: jax.ShapeDtypeStruct((n_pad, MOE_TOPK), dt)
    pair_spec = pl.BlockSpec((tm, MOE_TOPK), lambda i: (i, 0))
    return pl.pallas_call(
        functools.partial(_route_kernel, n_valid=n_valid),
        grid=(n_pad // tm,),
        in_specs=[pl.BlockSpec((tm, ROUTER_COLS), lambda i: (i, 0))],
        out_specs=[pair_spec, pair_spec, pair_spec, pl.BlockSpec((1, ROUTER_COLS), lambda i: (0, 0))],
        out_shape=[pair(jnp.int32), pair(F32), pair(jnp.int32), jax.ShapeDtypeStruct((1, ROUTER_COLS), F32)],
        scratch_shapes=[pltpu.VMEM((1, ROUTER_COLS), F32)],
        compiler_params=_params("arbitrary"),
        name="moe_route",
    )(logits)


def _dispatch(e_ids, rank, lane_counts):
    n = e_ids.shape[0]
    tb = MOE_BLOCK
    counts = lane_counts[0, MOE_GROUPS:MOE_GROUPS + MOE_EXPERTS].astype(jnp.int32)
    padded = ((counts + tb - 1) // tb) * tb
    pend = jnp.cumsum(padded)
    pstart = pend - padded
    dest = pstart[e_ids] + rank
    n_blocks = (n * MOE_TOPK + MOE_EXPERTS * (tb - 1) + tb - 1) // tb
    tok = jnp.repeat(jnp.arange(n, dtype=jnp.int32), MOE_TOPK)
    buf_tok = jnp.zeros((n_blocks * tb,), jnp.int32).at[dest.reshape(-1)].set(tok)
    block_e = jnp.minimum(jnp.searchsorted(pend, jnp.arange(n_blocks, dtype=jnp.int32) * tb, side='right'),
                          MOE_EXPERTS - 1).astype(jnp.int32)
    n_used = (pend[-1] // tb).astype(jnp.int32)
    idx = jnp.arange(n_blocks, dtype=jnp.int32)
    first = ((idx < n_used) & ((idx == 0) | (block_e != jnp.roll(block_e, 1)))).astype(jnp.int32)
    after = (pend[block_e] // tb).astype(jnp.int32)
    next_e = jnp.where(after < n_used, block_e[jnp.minimum(after, n_blocks - 1)], -1).astype(jnp.int32)
    return dest, buf_tok, block_e, n_used.reshape(1), first, next_e


def _combine_kernel(slot_ref, x_ref, ga_ref, w_ref, y_hbm, *rest, tok0, final):
    if final:
        g_ref, o_ref, y_buf, sem = rest
    else:
        o_ref, y_buf, sem = rest
    tm = x_ref.shape[0]
    base = (tok0 + pl.program_id(0) * tm) * MOE_TOPK

    def row_copy(r, k):
        return pltpu.make_async_copy(y_hbm.at[pl.ds(slot_ref[base + r * MOE_TOPK + k], 1)],
                                     y_buf.at[k, pl.ds(r, 1)], sem.at[k])

    def start(r, c):
        for k in range(MOE_TOPK):
            row_copy(r, k).start(priority=k % 2)
        return c

    def wait(r, c):
        for k in range(MOE_TOPK):
            row_copy(r, k).wait()
        return c

    lax.fori_loop(0, tm, start, 0, unroll=8)
    lax.fori_loop(0, tm, wait, 0, unroll=8)
    w = w_ref[...]
    out = x_ref[...] + ga_ref[...] * (y_buf[0] * w[:, 0:1] + y_buf[1] * w[:, 1:2])
    if final:
        ms = jnp.mean(out * out, axis=-1, keepdims=True)
        out = (out * lax.rsqrt(ms + RMS_EPS)) * g_ref[...]
    o_ref[...] = out


def moe_combine(x, mods, y, slot_flat, gate_w, tok0, tm, rows_per_batch, final_gain=None):
    m = x.shape[0]
    final = final_gain is not None
    if mods.ndim == 3:
        ga_spec = pl.BlockSpec((None, 1, D_MODEL), lambda i, s: ((i * tm) // rows_per_batch, 0, 5))
    else:
        ga_spec = pl.BlockSpec((tm, D_MODEL), lambda i, s: (i, 5))
    row_spec = pl.BlockSpec((tm, D_MODEL), lambda i, s: (i, 0))
    in_specs = [row_spec, ga_spec,
                pl.BlockSpec((tm, MOE_TOPK), lambda i, s: (tok0 // tm + i, 0)),
                pl.BlockSpec(memory_space=pl.ANY)]
    args = [x, mods, gate_w, y]
    if final:
        in_specs.append(pl.BlockSpec((1, D_MODEL), lambda i, s: (0, 0)))
        args.append(final_gain.reshape(1, D_MODEL))
    grid_spec = pltpu.PrefetchScalarGridSpec(
        num_scalar_prefetch=1,
        grid=(m // tm,),
        in_specs=in_specs,
        out_specs=row_spec,
        scratch_shapes=[pltpu.VMEM((MOE_TOPK, tm, D_MODEL), F32), pltpu.SemaphoreType.DMA((MOE_TOPK,))],
    )
    return pl.pallas_call(
        functools.partial(_combine_kernel, tok0=tok0, final=final),
        grid_spec=grid_spec,
        out_shape=jax.ShapeDtypeStruct((m, D_MODEL), F32),
        compiler_params=_params("arbitrary"),
        name="moe_combine",
    )(slot_flat, *args)


def kernel(x_prompt, x_sample, c_prompt, c_sample, state_gla, cache_k, cache_v, cache_logf, page_table, norm_mix, norm_ffn, norm_final, w_ada, b_ada, gla_w_in, gla_w_gate_up, gla_b_gate, gla_norm, gla_w_out, fox_w_in, fox_b_f, fox_q_norm, fox_k_norm, fox_w_out, moe_w_group, moe_b_group, moe_w_expert, moe_b_expert, moe_w_gate, moe_w_up, moe_w_down):
    bsz, seq, d = x_prompt.shape
    db, ds, _ = x_sample.shape
    depth = w_ada.shape[0]
    mp, msz = bsz * seq, db * ds
    ti_p = min(1024, seq)
    tm_p = min(512, seq)
    te_p = min(256, seq)
    n_tok = mp + msz
    n_route = -(-n_tok // ROUTE_BLOCK) * ROUTE_BLOCK
    assert d == D_MODEL and seq % ti_p == 0 and seq % GLA_CHUNK == 0 and seq % ATTN_BLOCK == 0
    assert ds <= GLA_CHUNK and page_table.shape[1] % PAGES_PER_STEP == 0
    assert page_table.shape[1] % SUFFIX_PAGES_PER_STEP == 0


    c_rows = jnp.concatenate([c_prompt, c_sample], axis=0)
    c_rows = jnp.pad(c_rows, ((0, (-c_rows.shape[0]) % 8), (0, 0)))
    ada = ada_all(c_rows, w_ada, b_ada)

    xp = x_prompt.reshape(mp, d)
    xs = x_sample.reshape(msz, d)
    gla_p, gla_s, kp_l, vp_l, lfp_l, ks_l, vs_l, lfs_l = [], [], [], [], [], [], [], []
    for i in range(depth):
        j = i // 2
        mods_p = ada[i, :bsz].reshape(bsz, 1, 6 * d)
        mods_s = jnp.repeat(ada[i, bsz:bsz + db], ds, axis=0)
        if i % 2 == 0:
            zp, lgp = gla_in_proj(xp, norm_mix[i], mods_p, gla_w_in, j, gla_w_gate_up[j], gla_b_gate[j],
                                  ti_p, seq)
            s0 = jnp.zeros((bsz, GLA_HEADS, GLA_DKH, GLA_DVH), F32)
            op, s_fin = gla_scan(zp.reshape(bsz, seq, GLA_MAIN), lgp.reshape(bsz, seq, GLA_DK), s0)
            xp = out_proj(op.reshape(mp, d), zp, 2, gla_norm[j], gla_w_out, j, xp, mods_p, tm_p, seq)
            gla_p.append(s_fin)

            zs, lgs = gla_in_proj(xs, norm_mix[i], mods_s, gla_w_in, j, gla_w_gate_up[j], gla_b_gate[j],
                                  msz, msz)
            pad = ((0, 0), (0, GLA_CHUNK - ds), (0, 0))
            zs_pad = jnp.pad(zs.reshape(db, ds, GLA_MAIN), pad)
            lgs_pad = jnp.pad(lgs.reshape(db, ds, GLA_DK), pad)
            os_pad, s_new = gla_scan(zs_pad, lgs_pad, state_gla[j])
            xs = out_proj(os_pad[:, :ds].reshape(msz, d), zs, 2, gla_norm[j], gla_w_out, j, xs, mods_s,
                          msz, msz)
            gla_s.append(s_new)
        else:
            qp, kp, vp, gp, lfp = fox_in_proj(xp, norm_mix[i], mods_p, fox_w_in, j, fox_b_f[j],
                                              fox_q_norm[j], fox_k_norm[j], ti_p, seq)
            rows3 = (bsz, seq, FOX_DIM)
            lf3 = lfp.reshape(bsz, seq, FOX_HEADS)
            op = fox_prompt_attn(qp.reshape(rows3), kp.reshape(rows3), vp.reshape(rows3), cumsum_time(lf3))
            xp = out_proj(op.reshape(mp, d), gp, 0, None, fox_w_out, j, xp, mods_p, tm_p, seq)
            kp_l.append(kp.reshape(bsz, seq, FOX_HEADS, FOX_HD))
            vp_l.append(vp.reshape(bsz, seq, FOX_HEADS, FOX_HD))
            lfp_l.append(lf3)

            qs, ks, vs, gs, lfs = fox_in_proj(xs, norm_mix[i], mods_s, fox_w_in, j, fox_b_f[j],
                                              fox_q_norm[j], fox_k_norm[j], msz, msz)
            heads = (db, ds, FOX_HEADS, FOX_HD)
            lfs3 = lfs.reshape(db, ds, FOX_HEADS)
            os_ = fox_paged_attn(qs.reshape(heads), ks.reshape(heads), vs.reshape(heads), lfs3,
                                 cache_k, cache_v, cache_logf, j, page_table)
            xs = out_proj(os_.reshape(msz, d), gs, 0, None, fox_w_out, j, xs, mods_s, msz, msz)
            ks_l.append(ks.reshape(heads))
            vs_l.append(vs.reshape(heads))
            lfs_l.append(lfs3)

        w_router = jnp.pad(jnp.concatenate([moe_w_group[i], moe_w_expert[i]], axis=1),
                           ((0, 0), (0, ROUTER_COLS - MOE_GROUPS - MOE_EXPERTS))).astype(BF16)
        b_router = jnp.pad(jnp.concatenate([moe_b_group[i], moe_b_expert[i]]),
                           (0, ROUTER_COLS - MOE_GROUPS - MOE_EXPERTS)).reshape(1, ROUTER_COLS)
        h_all, lop = moe_pre(xp, norm_ffn[i], mods_p, w_router, b_router, tm_p, seq, n_tok, 0)
        h_all, los = moe_pre(xs, norm_ffn[i], mods_s, w_router, b_router, msz, msz, n_tok, mp, h_all)
        logits = jnp.concatenate([lop, los, jnp.zeros((n_route - n_tok, ROUTER_COLS), F32)], axis=0)
        e_ids, gate_w, rank, lane_counts = moe_route(logits, n_tok)
        slot, buf_tok, block_e, n_used, first, next_e = _dispatch(e_ids[:n_tok], rank[:n_tok], lane_counts)
        y = moe_experts(h_all, buf_tok, block_e, n_used, first, next_e, moe_w_gate, moe_w_up, moe_w_down, i)
        slot_flat = slot.reshape(-1)
        closing = norm_final if i == depth - 1 else None
        xp = moe_combine(xp, mods_p, y, slot_flat, gate_w, 0, te_p, seq, closing)
        xs = moe_combine(xs, mods_s, y, slot_flat, gate_w, mp, msz, msz, closing)

    y_prompt = xp.reshape(bsz, seq, d)
    y_sample = xs.reshape(db, ds, d)
    return (y_prompt, y_sample,
            jnp.stack(kp_l), jnp.stack(vp_l), jnp.stack(lfp_l), jnp.stack(gla_p),
            jnp.stack(ks_l), jnp.stack(vs_l), jnp.stack(lfs_l), jnp.stack(gla_s))
```

```python
import functools

import jax
import jax.numpy as jnp
from jax import lax
from jax.experimental import pallas as pl
from jax.experimental.pallas import tpu as pltpu

F32 = jnp.float32
BF16 = jnp.bfloat16

D_MODEL = 2048
GLA_HEADS = 4
GLA_DK = D_MODEL // 2
GLA_DV = D_MODEL
GLA_DKH = GLA_DK // GLA_HEADS
GLA_DVH = GLA_DV // GLA_HEADS
GLA_GATE_RANK = 16
GLA_GATE_NORM = 16.0
GLA_MAIN = 2 * GLA_DK + 2 * GLA_DV
FOX_HEADS = 16
FOX_HD = D_MODEL // FOX_HEADS
FOX_DIM = FOX_HEADS * FOX_HD
MOE_GROUPS = 4
MOE_PER_GROUP = 8
MOE_EXPERTS = MOE_GROUPS * MOE_PER_GROUP
MOE_TOPK = 2
MOE_DFF = D_MODEL // 4
RMS_EPS = 1e-6
NEG_INF = -1e30

VMEM_LIMIT_BYTES = 52 * 1024 * 1024
LANES = 128

GLA_CHUNK = 128
GLA_SUB = 16
ATTN_BLOCK = 256
PAGES_PER_STEP = 8
SUFFIX_PAGES_PER_STEP = 16
ROUTE_BLOCK = 256
MOE_BLOCK = 256
ROUTER_COLS = 128


def _params(*sem):
    return pltpu.CompilerParams(dimension_semantics=sem, vmem_limit_bytes=VMEM_LIMIT_BYTES)


def _log_sigmoid(x):
    return jnp.minimum(x, 0.0) - jnp.log1p(jnp.exp(-jnp.abs(x)))


def _silu(x):
    return x * jax.nn.sigmoid(x)


def _split3(a):
    hi = a.astype(BF16)
    r1 = a - hi.astype(F32)
    mid = r1.astype(BF16)
    lo = (r1 - mid.astype(F32)).astype(BF16)
    return hi, mid, lo


def _dot_exact_lhs(sel_bf16, x_f32):
    hi, mid, lo = _split3(x_f32)
    d = functools.partial(jnp.dot, preferred_element_type=F32)
    return d(sel_bf16, hi) + d(sel_bf16, mid) + d(sel_bf16, lo)


def _modulate(x, g, shift, scale):
    ms = jnp.mean(x * x, axis=-1, keepdims=True)
    return (x * lax.rsqrt(ms + RMS_EPS)) * g * (1.0 + scale) + shift


def _ada_kernel(c_ref, w_ref, b_ref, o_ref):
    a = _silu(c_ref[...]).astype(BF16)
    o_ref[...] = jnp.dot(a, w_ref[...].astype(BF16), preferred_element_type=F32) + b_ref[...]


def ada_all(c_rows, w_ada, b_ada, tn=1024):
    depth, d, n = w_ada.shape
    rows = c_rows.shape[0]
    return pl.pallas_call(
        _ada_kernel,
        grid=(depth, n // tn),
        in_specs=[
            pl.BlockSpec((rows, d), lambda l, j: (0, 0)),
            pl.BlockSpec((None, d, tn), lambda l, j: (l, 0, j)),
            pl.BlockSpec((None, 1, tn), lambda l, j: (l, 0, j)),
        ],
        out_specs=pl.BlockSpec((None, rows, tn), lambda l, j: (l, 0, j)),
        out_shape=jax.ShapeDtypeStruct((depth, rows, n), F32),
        compiler_params=_params("arbitrary", "arbitrary"),
        name="ada_all",
    )(c_rows, w_ada, b_ada.reshape(depth, 1, n))


def _mod_specs(mods, chunk_ids, tm, rows_per_batch):
    specs = []
    for c in chunk_ids:
        if mods.ndim == 3:
            specs.append(pl.BlockSpec((None, 1, D_MODEL),
                                      lambda i, j, c=c: ((i * tm) // rows_per_batch, 0, c)))
        else:
            specs.append(pl.BlockSpec((tm, D_MODEL), lambda i, j, c=c: (i, c)))
    return specs


def _gla_in_kernel(x_ref, g_ref, sh_ref, sc_ref, w_ref, wgd_ref, wup_ref, bg_ref, z_ref, lg_ref, h_scr):
    @pl.when(pl.program_id(1) == 0)
    def _():
        h = _modulate(x_ref[...], g_ref[...], sh_ref[...], sc_ref[...]).astype(BF16)
        h_scr[...] = h
        gd = jnp.dot(h, wgd_ref[...].astype(BF16), preferred_element_type=F32)
        gate = jnp.dot(gd.astype(BF16), wup_ref[...].astype(BF16), preferred_element_type=F32) + bg_ref[...]
        lg_ref[...] = _log_sigmoid(gate) * (1.0 / GLA_GATE_NORM)

    z_ref[...] = jnp.dot(h_scr[...], w_ref[...].astype(BF16), preferred_element_type=F32)


def gla_in_proj(x, gain, mods, w_in_all, layer, w_gate_up, b_gate, tm, rows_per_batch, tn=512):
    m = x.shape[0]
    w_gd = w_in_all[layer, :, GLA_MAIN:]
    sh_spec, sc_spec = _mod_specs(mods, (0, 1), tm, rows_per_batch)
    return pl.pallas_call(
        _gla_in_kernel,
        grid=(m // tm, GLA_MAIN // tn),
        in_specs=[
            pl.BlockSpec((tm, D_MODEL), lambda i, j: (i, 0)),
            pl.BlockSpec((1, D_MODEL), lambda i, j: (0, 0)),
            sh_spec, sc_spec,
            pl.BlockSpec((None, D_MODEL, tn), lambda i, j: (layer, 0, j)),
            pl.BlockSpec((D_MODEL, GLA_GATE_RANK), lambda i, j: (0, 0)),
            pl.BlockSpec((GLA_GATE_RANK, GLA_DK), lambda i, j: (0, 0)),
            pl.BlockSpec((1, GLA_DK), lambda i, j: (0, 0)),
        ],
        out_specs=[
            pl.BlockSpec((tm, tn), lambda i, j: (i, j)),
            pl.BlockSpec((tm, GLA_DK), lambda i, j: (i, 0)),
        ],
        out_shape=[jax.ShapeDtypeStruct((m, GLA_MAIN), F32), jax.ShapeDtypeStruct((m, GLA_DK), F32)],
        scratch_shapes=[pltpu.VMEM((tm, D_MODEL), BF16)],
        compiler_params=_params("arbitrary", "arbitrary"),
        name="gla_in_proj",
    )(x, gain.reshape(1, D_MODEL), mods, mods, w_in_all, w_gd, w_gate_up, b_gate.reshape(1, GLA_DK))


def _gla_scan_kernel(q_ref, k_ref, v_ref, lg_ref, s0_ref, o_ref, s_ref):
    C, R = GLA_CHUNK, GLA_SUB

    @pl.when(pl.program_id(2) == 0)
    def _():
        s_ref[...] = s0_ref[...]

    q = q_ref[...] * (GLA_DKH ** -0.5)
    k = k_ref[...]
    v = v_ref[...]
    vb = v.astype(BF16)
    row = lax.broadcasted_iota(jnp.int32, (C, C), 0)
    col = lax.broadcasted_iota(jnp.int32, (C, C), 1)
    tri = (row >= col).astype(BF16)
    b = _dot_exact_lhs(tri, lg_ref[...])
    state = s_ref[...]
    inter = jnp.dot((q * jnp.exp(b)).astype(BF16), state.astype(BF16), preferred_element_type=F32)

    t_idx = lax.broadcasted_iota(jnp.int32, (R, 1), 0)
    s_idx = lax.broadcasted_iota(jnp.int32, (R, R), 1)
    for i in range(C // R):
        lo = i * R
        bi = b[lo:lo + R]
        qi = q[lo:lo + R]
        ki = k[lo:lo + R]
        oi = inter[lo:lo + R]
        if i > 0:
            b_ref_row = b[lo - 1:lo]
            qe = (qi * jnp.exp(bi - b_ref_row)).astype(BF16)
            ke = (k[:lo] * jnp.exp(b_ref_row - b[:lo])).astype(BF16)
            a = lax.dot_general(qe, ke, (((1,), (1,)), ((), ())), preferred_element_type=F32)
            oi = oi + jnp.dot(a.astype(BF16), vb[:lo], preferred_element_type=F32)
        a_diag = jnp.zeros((R, R), F32)
        for s in range(R):
            rel = jnp.where(t_idx >= s, bi - bi[s:s + 1], NEG_INF)
            w = jnp.sum(qi * ki[s:s + 1] * jnp.exp(rel), axis=-1, keepdims=True)
            a_diag = jnp.where(s_idx == s, w, a_diag)
        oi = oi + jnp.dot(a_diag.astype(BF16), vb[lo:lo + R], preferred_element_type=F32)
        o_ref[lo:lo + R, :] = oi

    b_t = b.T
    b_last = b_t[:, C - 1:C]
    ke_t = (k.T * jnp.exp(b_last - b_t)).astype(BF16)
    s_ref[...] = jnp.exp(b_last) * state + jnp.dot(ke_t, vb, preferred_element_type=F32)


def gla_scan(z, lg, s0):
    bsz, t, _ = z.shape
    C = GLA_CHUNK
    kq = GLA_DK // GLA_DKH
    return pl.pallas_call(
        _gla_scan_kernel,
        grid=(bsz, GLA_HEADS, t // C),
        in_specs=[
            pl.BlockSpec((None, C, GLA_DKH), lambda b, h, c: (b, c, h)),
            pl.BlockSpec((None, C, GLA_DKH), lambda b, h, c: (b, c, kq + h)),
            pl.BlockSpec((None, C, GLA_DVH), lambda b, h, c: (b, c, (2 * GLA_DK) // GLA_DVH + h)),
            pl.BlockSpec((None, C, GLA_DKH), lambda b, h, c: (b, c, h)),
            pl.BlockSpec((None, None, GLA_DKH, GLA_DVH), lambda b, h, c: (b, h, 0, 0)),
        ],
        out_specs=[
            pl.BlockSpec((None, C, GLA_DVH), lambda b, h, c: (b, c, h)),
            pl.BlockSpec((None, None, GLA_DKH, GLA_DVH), lambda b, h, c: (b, h, 0, 0)),
        ],
        out_shape=[jax.ShapeDtypeStruct((bsz, t, GLA_DV), F32),
                   jax.ShapeDtypeStruct((bsz, GLA_HEADS, GLA_DKH, GLA_DVH), F32)],
        compiler_params=_params("arbitrary", "arbitrary", "arbitrary"),
        name="gla_scan",
    )(z, z, z, lg, s0)


def _gla_out_kernel(o_ref, g_ref, ng_ref, w_ref, x_ref, ga_ref, y_ref, p_scr):
    @pl.when(pl.program_id(1) == 0)
    def _():
        gate = _silu(g_ref[...])
        for h in range(GLA_HEADS):
            sl = slice(h * GLA_DVH, (h + 1) * GLA_DVH)
            o = o_ref[:, sl]
            ms = jnp.mean(o * o, axis=-1, keepdims=True)
            p_scr[:, sl] = ((o * lax.rsqrt(ms + RMS_EPS)) * ng_ref[...] * gate[:, sl]).astype(BF16)

    y_ref[...] = x_ref[...] + ga_ref[...] * jnp.dot(p_scr[...], w_ref[...].astype(BF16),
                                                    preferred_element_type=F32)


def _fox_out_kernel(o_ref, g_ref, w_ref, x_ref, ga_ref, y_ref, p_scr):
    @pl.when(pl.program_id(1) == 0)
    def _():
        p_scr[...] = (o_ref[...] * jax.nn.sigmoid(g_ref[...])).astype(BF16)

    y_ref[...] = x_ref[...] + ga_ref[...] * jnp.dot(p_scr[...], w_ref[...].astype(BF16),
                                                    preferred_element_type=F32)


def out_proj(o, z, g_block, norm_gain, w_out_all, layer, x, mods, tm, rows_per_batch, tn=512):
    m = x.shape[0]
    row_spec = pl.BlockSpec((tm, D_MODEL), lambda i, j: (i, 0))
    in_specs = [row_spec, pl.BlockSpec((tm, D_MODEL), lambda i, j: (i, g_block))]
    args = [o, z]
    if norm_gain is not None:
        in_specs.append(pl.BlockSpec((1, GLA_DVH), lambda i, j: (0, 0)))
        args.append(norm_gain.reshape(1, GLA_DVH))
        body = _gla_out_kernel
    else:
        body = _fox_out_kernel
    in_specs += [pl.BlockSpec((None, D_MODEL, tn), lambda i, j: (layer, 0, j)),
                 pl.BlockSpec((tm, tn), lambda i, j: (i, j))]
    args += [w_out_all, x]
    if mods.ndim == 3:
        ga_spec = pl.BlockSpec((None, 1, tn),
                               lambda i, j: ((i * tm) // rows_per_batch, 0, 2 * (D_MODEL // tn) + j))
    else:
        ga_spec = pl.BlockSpec((tm, tn), lambda i, j: (i, 2 * (D_MODEL // tn) + j))
    in_specs.append(ga_spec)
    args.append(mods)
    return pl.pallas_call(
        body,
        grid=(m // tm, D_MODEL // tn),
        in_specs=in_specs,
        out_specs=pl.BlockSpec((tm, tn), lambda i, j: (i, j)),
        out_shape=jax.ShapeDtypeStruct((m, D_MODEL), F32),
        scratch_shapes=[pltpu.VMEM((tm, D_MODEL), BF16)],
        compiler_params=_params("arbitrary", "arbitrary"),
        name="out_proj",
    )(*args)


def _fox_in_kernel(x_ref, g_ref, sh_ref, sc_ref, w_ref, wf_ref, bf_ref, qn_ref, kn_ref,
                   q_ref, k_ref, v_ref, go_ref, lf_ref, h_scr, *, tn):
    j = pl.program_id(1)
    nt = FOX_DIM // tn

    @pl.when(j == 0)
    def _():
        h = _modulate(x_ref[...], g_ref[...], sh_ref[...], sc_ref[...]).astype(BF16)
        h_scr[...] = h
        f = jnp.dot(h, wf_ref[...].astype(BF16), preferred_element_type=F32)
        lf_ref[...] = _log_sigmoid(f + bf_ref[...])

    acc = jnp.dot(h_scr[...], w_ref[...].astype(BF16), preferred_element_type=F32)

    def head_norm(o_ref, gain):
        for c in range(tn // FOX_HD):
            blk = acc[:, c * FOX_HD:(c + 1) * FOX_HD]
            ms = jnp.mean(blk * blk, axis=-1, keepdims=True)
            o_ref[:, c * FOX_HD:(c + 1) * FOX_HD] = (blk * lax.rsqrt(ms + RMS_EPS)) * gain

    @pl.when(j < nt)
    def _():
        head_norm(q_ref, qn_ref[...])

    @pl.when((j >= nt) & (j < 2 * nt))
    def _():
        head_norm(k_ref, kn_ref[...])

    @pl.when((j >= 2 * nt) & (j < 3 * nt))
    def _():
        v_ref[...] = acc

    @pl.when(j >= 3 * nt)
    def _():
        go_ref[...] = acc


def fox_in_proj(x, gain, mods, w_in_all, layer, b_f, q_norm, k_norm, tm, rows_per_batch, tn=512):
    m = x.shape[0]
    n_main = 4 * FOX_DIM
    nt = FOX_DIM // tn
    w_f = w_in_all[layer, :, n_main:]
    sh_spec, sc_spec = _mod_specs(mods, (0, 1), tm, rows_per_batch)
    col = lambda j, g: jnp.clip(j - g * nt, 0, nt - 1)
    flat_spec = lambda g: pl.BlockSpec((tm, tn), lambda i, j: (i, col(j, g)))
    flat = jax.ShapeDtypeStruct((m, FOX_DIM), F32)
    return pl.pallas_call(
        functools.partial(_fox_in_kernel, tn=tn),
        grid=(m // tm, n_main // tn),
        in_specs=[
            pl.BlockSpec((tm, D_MODEL), lambda i, j: (i, 0)),
            pl.BlockSpec((1, D_MODEL), lambda i, j: (0, 0)),
            sh_spec, sc_spec,
            pl.BlockSpec((None, D_MODEL, tn), lambda i, j: (layer, 0, j)),
            pl.BlockSpec((D_MODEL, FOX_HEADS), lambda i, j: (0, 0)),
            pl.BlockSpec((1, FOX_HEADS), lambda i, j: (0, 0)),
            pl.BlockSpec((1, FOX_HD), lambda i, j: (0, 0)),
            pl.BlockSpec((1, FOX_HD), lambda i, j: (0, 0)),
        ],
        out_specs=[flat_spec(0), flat_spec(1), flat_spec(2), flat_spec(3),
                   pl.BlockSpec((tm, FOX_HEADS), lambda i, j: (i, 0))],
        out_shape=[flat, flat, flat, flat, jax.ShapeDtypeStruct((m, FOX_HEADS), F32)],
        scratch_shapes=[pltpu.VMEM((tm, D_MODEL), BF16)],
        compiler_params=_params("arbitrary", "arbitrary"),
        name="fox_in_proj",
    )(x, gain.reshape(1, D_MODEL), mods, mods, w_in_all, w_f, b_f.reshape(1, FOX_HEADS),
      q_norm.reshape(1, FOX_HD), k_norm.reshape(1, FOX_HD))


def _cumsum_kernel(x_ref, o_ref):
    t = x_ref.shape[0]
    blk = LANES
    row = lax.broadcasted_iota(jnp.int32, (blk, blk), 0)
    col = lax.broadcasted_iota(jnp.int32, (blk, blk), 1)
    tri = (row >= col).astype(BF16)
    carry = jnp.zeros((1, x_ref.shape[1]), F32)
    for i in range(t // blk):
        c = _dot_exact_lhs(tri, x_ref[i * blk:(i + 1) * blk, :]) + carry
        o_ref[i * blk:(i + 1) * blk, :] = c
        carry = c[blk - 1:blk]


def cumsum_time(x):
    bsz, t, h = x.shape
    return pl.pallas_call(
        _cumsum_kernel,
        grid=(bsz,),
        in_specs=[pl.BlockSpec((None, t, h), lambda b: (b, 0, 0))],
        out_specs=pl.BlockSpec((None, t, h), lambda b: (b, 0, 0)),
        out_shape=jax.ShapeDtypeStruct((bsz, t, h), F32),
        compiler_params=_params("arbitrary"),
        name="cumsum_time",
    )(x)


def _fox_attn_kernel(q_ref, k_ref, v_ref, cc_ref, cr_ref, o_ref):
    blk = ATTN_BLOCK
    t = q_ref.shape[0]
    nt_dims = (((1,), (1,)), ((), ()))
    kb = k_ref[...].astype(BF16)
    vb = v_ref[...].astype(BF16)
    row = lax.broadcasted_iota(jnp.int32, (blk, blk), 0)
    col = lax.broadcasted_iota(jnp.int32, (blk, blk), 1)
    for i in range(t // blk):
        lo = i * blk
        q = (q_ref[lo:lo + blk, :] * (FOX_HD ** -0.5)).astype(BF16)
        cq = cc_ref[lo:lo + blk, :]
        s_d = lax.dot_general(q, kb[lo:lo + blk], nt_dims, preferred_element_type=F32)
        s_d = jnp.where(col <= row, s_d + cq - cr_ref[:, lo:lo + blk], NEG_INF)
        m = jnp.max(s_d, axis=-1, keepdims=True)
        if i > 0:
            s_p = lax.dot_general(q, kb[:lo], nt_dims, preferred_element_type=F32) + cq - cr_ref[:, :lo]
            m = jnp.maximum(m, jnp.max(s_p, axis=-1, keepdims=True))
        p_d = jnp.exp(s_d - m)
        l = jnp.sum(p_d, axis=-1, keepdims=True)
        acc = jnp.dot(p_d.astype(BF16), vb[lo:lo + blk], preferred_element_type=F32)
        if i > 0:
            p_p = jnp.exp(s_p - m)
            l = l + jnp.sum(p_p, axis=-1, keepdims=True)
            acc = acc + jnp.dot(p_p.astype(BF16), vb[:lo], preferred_element_type=F32)
        o_ref[lo:lo + blk, :] = acc / l


def fox_prompt_attn(q, k, v, cum):
    bsz, t, _ = q.shape
    cum_h = jnp.transpose(cum, (0, 2, 1))
    cum_col = cum_h.reshape(bsz, FOX_HEADS, t, 1)
    cum_row = cum_h.reshape(bsz, FOX_HEADS, 1, t)
    head_spec = pl.BlockSpec((None, t, FOX_HD), lambda b, h: (b, 0, h))
    return pl.pallas_call(
        _fox_attn_kernel,
        grid=(bsz, FOX_HEADS),
        in_specs=[
            head_spec, head_spec, head_spec,
            pl.BlockSpec((None, None, t, 1), lambda b, h: (b, h, 0, 0)),
            pl.BlockSpec((None, None, 1, t), lambda b, h: (b, h, 0, 0)),
        ],
        out_specs=head_spec,
        out_shape=jax.ShapeDtypeStruct((bsz, t, FOX_DIM), F32),
        compiler_params=_params("arbitrary", "arbitrary"),
        name="fox_prompt_attn",
    )(q, k, v, cum_col, cum_row)


def _fox_suffix_kernel(pt_ref, *rest):
    G = SUFFIX_PAGES_PER_STEP
    lf_refs = rest[:G]
    o_ref, carry_scr = rest[G:]
    width = lf_refs[0].shape[1]

    @pl.when(pl.program_id(1) == 0)
    def _():
        carry_scr[...] = jnp.zeros(carry_scr.shape, F32)

    lf = jnp.concatenate([r[...] for r in lf_refs], axis=0)
    lane = lax.broadcasted_iota(jnp.int32, (G, width), 1)
    inc = lf
    s = FOX_HEADS
    while s < width:
        inc = inc + jnp.where(lane + s < width, pltpu.roll(inc, width - s, axis=1), 0.0)
        s *= 2
    tot = jnp.where(lane < FOX_HEADS, inc, 0.0)
    s = FOX_HEADS
    while s < width:
        tot = tot + pltpu.roll(tot, s, axis=1)
        s *= 2
    later = inc - lf
    carry = carry_scr[...]
    for g in range(G):
        o_ref[G - 1 - g] = later[g:g + 1] + carry
        carry = carry + tot[g:g + 1]
    carry_scr[...] = carry


def fox_suffix(cache_logf, layer, page_table):
    db, n_pages = page_table.shape
    n_fox, n_pool, page, _ = cache_logf.shape
    width = page * FOX_HEADS
    G = SUFFIX_PAGES_PER_STEP
    steps = n_pages // G
    lf_flat = cache_logf.reshape(n_fox, n_pool, 1, width)
    lf_specs = [pl.BlockSpec((None, None, 1, width),
                             lambda b, t, pt, g=g: (layer, pt[b, n_pages - 1 - (t * G + g)], 0, 0))
                for g in range(G)]
    grid_spec = pltpu.PrefetchScalarGridSpec(
        num_scalar_prefetch=1,
        grid=(db, steps),
        in_specs=lf_specs,
        out_specs=pl.BlockSpec((None, G, 1, width), lambda b, t, pt: (b, steps - 1 - t, 0, 0)),
        scratch_shapes=[pltpu.VMEM((1, width), F32)],
    )
    return pl.pallas_call(
        _fox_suffix_kernel,
        grid_spec=grid_spec,
        out_shape=jax.ShapeDtypeStruct((db, n_pages, 1, width), F32),
        compiler_params=_params("arbitrary", "arbitrary"),
        name="fox_suffix",
    )(page_table, *([lf_flat] * G))


def _fox_paged_kernel(pt_ref, q_ref, ct_ref, kn_ref, vn_ref, bn_ref, suf_ref, *rest):
    G = PAGES_PER_STEP
    k_refs = rest[:G]
    v_refs = rest[G:2 * G]
    o_ref = rest[2 * G]
    m_scr, l_scr, acc_scr = rest[2 * G + 1:]
    t = pl.program_id(1)
    page = k_refs[0].shape[0]
    rows = q_ref.shape[0]
    nt_dims = (((1,), (1,)), ((), ()))

    @pl.when(t == 0)
    def _():
        m_scr[...] = jnp.full(m_scr.shape, NEG_INF, F32)
        l_scr[...] = jnp.zeros(l_scr.shape, F32)
        acc_scr[...] = jnp.zeros(acc_scr.shape, F32)

    q = q_ref[...].astype(BF16)
    r_id = lax.broadcasted_iota(jnp.int32, (rows, LANES), 0)
    c_id = lax.broadcasted_iota(jnp.int32, (rows, LANES), 1)
    head_bias = jnp.where(r_id % FOX_HEADS == c_id % FOX_HEADS, 0.0, NEG_INF)
    head_bias = jnp.concatenate([head_bias] * (page * FOX_HEADS // LANES), axis=1) + ct_ref[...]

    def online(carry, s_list, v_list):
        m, l, acc = carry
        m_new = m
        for s in s_list:
            m_new = jnp.maximum(m_new, jnp.max(s, axis=-1, keepdims=True))
        alpha = jnp.exp(m - m_new)
        l = alpha * l
        acc = alpha * acc
        for s, v2 in zip(s_list, v_list):
            p = jnp.exp(s - m_new)
            l = l + jnp.sum(p, axis=-1, keepdims=True)
            acc = acc + jnp.dot(p.astype(BF16), v2, preferred_element_type=F32)
        return m_new, l, acc

    s_list, v_list = [], []
    for g in range(G):
        k2 = k_refs[g][...].reshape(page * FOX_HEADS, FOX_HD).astype(BF16)
        v_list.append(v_refs[g][...].reshape(page * FOX_HEADS, FOX_HD).astype(BF16))
        s_list.append(lax.dot_general(q, k2, nt_dims, preferred_element_type=F32) + (head_bias + suf_ref[g]))
    carry = online((m_scr[...], l_scr[...], acc_scr[...]), s_list, v_list)
    m_scr[...], l_scr[...], acc_scr[...] = carry

    @pl.when(t == pl.num_programs(1) - 1)
    def _():
        s = lax.dot_general(q, kn_ref[...].astype(BF16), nt_dims, preferred_element_type=F32) + bn_ref[...]
        _, l, acc = online(carry, [s], [vn_ref[...].astype(BF16)])
        o_ref[...] = acc / l


def fox_paged_attn(q, k_new, v_new, logf_new, cache_k, cache_v, cache_logf, layer, page_table):
    db, s_len, _, _ = q.shape
    n_pages = page_table.shape[1]
    page = cache_k.shape[2]
    G = PAGES_PER_STEP
    rows = s_len * FOX_HEADS
    new_cols = LANES
    assert rows % 8 == 0 and rows <= new_cols and (page * FOX_HEADS) % LANES == 0

    suf = fox_suffix(cache_logf, layer, page_table)

    q2 = (q * (FOX_HD ** -0.5)).reshape(db, rows, FOX_HD)
    cum = jnp.cumsum(logf_new, axis=1)
    ct = cum.reshape(db, rows, 1)
    same_head = jnp.arange(FOX_HEADS)[:, None] == jnp.arange(FOX_HEADS)[None, :]
    causal = jnp.arange(s_len)[None, :] <= jnp.arange(s_len)[:, None]
    ok = causal[:, None, :, None] & same_head[None, :, None, :]
    bn = jnp.where(ok[None], cum[:, :, :, None, None] - cum[:, None, None, :, :], NEG_INF)
    bn = jnp.pad(bn.reshape(db, rows, rows), ((0, 0), (0, 0), (0, new_cols - rows)), constant_values=NEG_INF)
    kn = jnp.pad(k_new.reshape(db, rows, FOX_HD), ((0, 0), (0, new_cols - rows), (0, 0)))
    vn = jnp.pad(v_new.reshape(db, rows, FOX_HD), ((0, 0), (0, new_cols - rows), (0, 0)))

    kv_specs = [pl.BlockSpec((None, None, page, FOX_HEADS, FOX_HD),
                             lambda b, t, pt, g=g: (layer, pt[b, t * G + g], 0, 0, 0)) for g in range(G)]
    grid_spec = pltpu.PrefetchScalarGridSpec(
        num_scalar_prefetch=1,
        grid=(db, n_pages // G),
        in_specs=[
            pl.BlockSpec((None, rows, FOX_HD), lambda b, t, pt: (b, 0, 0)),
            pl.BlockSpec((None, rows, 1), lambda b, t, pt: (b, 0, 0)),
            pl.BlockSpec((None, new_cols, FOX_HD), lambda b, t, pt: (b, 0, 0)),
            pl.BlockSpec((None, new_cols, FOX_HD), lambda b, t, pt: (b, 0, 0)),
            pl.BlockSpec((None, rows, new_cols), lambda b, t, pt: (b, 0, 0)),
            pl.BlockSpec((None, G, 1, page * FOX_HEADS), lambda b, t, pt: (b, t, 0, 0)),
        ] + kv_specs + kv_specs,
        out_specs=pl.BlockSpec((None, rows, FOX_HD), lambda b, t, pt: (b, 0, 0)),
        scratch_shapes=[pltpu.VMEM((rows, 1), F32), pltpu.VMEM((rows, 1), F32),
                        pltpu.VMEM((rows, FOX_HD), F32)],
    )
    o = pl.pallas_call(
        _fox_paged_kernel,
        grid_spec=grid_spec,
        out_shape=jax.ShapeDtypeStruct((db, rows, FOX_HD), F32),
        compiler_params=_params("arbitrary", "arbitrary"),
        name="fox_paged_attn",
    )(page_table, q2, ct, kn, vn, bn, suf, *([cache_k] * G), *([cache_v] * G))
    return o.reshape(db, s_len, FOX_HEADS, FOX_HD)


def _moe_pre_kernel(x_ref, g_ref, sh_ref, sc_ref, wr_ref, br_ref, *rest):
    h_ref, lo_ref = rest[-2:]
    h = _modulate(x_ref[...], g_ref[...], sh_ref[...], sc_ref[...])
    h_ref[...] = h
    lo_ref[...] = jnp.dot(h.astype(BF16), wr_ref[...], preferred_element_type=F32) + br_ref[...]


def moe_pre(x, gain, mods, w_router, b_router, tm, rows_per_batch, n_tok, tok0, h_prev=None):
    m = x.shape[0]
    tiles = m // tm
    prev = () if h_prev is None else (h_prev,)
    extra = 1 if (h_prev is None and n_tok > m) else 0
    assert n_tok - m <= tm or not extra
    src = lambda i: jnp.minimum(i, tiles - 1)
    if mods.ndim == 3:
        mod_spec = lambda c: pl.BlockSpec((None, 1, D_MODEL),
                                          lambda i, j: ((src(i) * tm) // rows_per_batch, 0, c))
    else:
        mod_spec = lambda c: pl.BlockSpec((tm, D_MODEL), lambda i, j: (src(i), c))
    return pl.pallas_call(
        _moe_pre_kernel,
        grid=(tiles + extra, 1),
        in_specs=[
            pl.BlockSpec((tm, D_MODEL), lambda i, j: (src(i), 0)),
            pl.BlockSpec((1, D_MODEL), lambda i, j: (0, 0)),
            mod_spec(3), mod_spec(4),
            pl.BlockSpec((D_MODEL, ROUTER_COLS), lambda i, j: (0, 0)),
            pl.BlockSpec((1, ROUTER_COLS), lambda i, j: (0, 0)),
        ] + [pl.BlockSpec(memory_space=pl.ANY)] * len(prev),
        out_specs=[
            pl.BlockSpec((tm, D_MODEL), lambda i, j: (tok0 // tm + i, 0)),
            pl.BlockSpec((tm, ROUTER_COLS), lambda i, j: (src(i), 0)),
        ],
        out_shape=[jax.ShapeDtypeStruct((n_tok, D_MODEL), F32), jax.ShapeDtypeStruct((m, ROUTER_COLS), F32)],
        input_output_aliases={6: 0} if prev else {},
        compiler_params=_params("arbitrary", "arbitrary"),
        name="moe_pre",
    )(x, gain.reshape(1, D_MODEL), mods, mods, w_router, b_router, *prev)


def _moe_expert_kernel(be_ref, nu_ref, tok_ref, first_ref, next_ref, h_hbm, wg_hbm, wu_hbm, wd_hbm, y_ref,
                       x_buf, sem, w_sem, wg_f, wu_f, wd_f, wg_s, wu_s, wd_s, *, layer):
    i = pl.program_id(0)
    tb = x_buf.shape[1]
    n_used = nu_ref[0]

    def weight_copies(e):
        return (pltpu.make_async_copy(wg_hbm.at[layer, e], wg_f, w_sem.at[0]),
                pltpu.make_async_copy(wu_hbm.at[layer, e], wu_f, w_sem.at[1]),
                pltpu.make_async_copy(wd_hbm.at[layer, e], wd_f, w_sem.at[2]))

    def row_copy(blk, r, half):
        return pltpu.make_async_copy(h_hbm.at[pl.ds(tok_ref[blk * tb + r], 1)],
                                     x_buf.at[half, pl.ds(r, 1)], sem.at[half])

    def start_rows(blk):
        def body(r, c):
            row_copy(blk, r, blk % 2).start()
            return c
        lax.fori_loop(0, tb, body, 0, unroll=8)

    def wait_rows(blk):
        def body(r, c):
            row_copy(blk, r, blk % 2).wait()
            return c
        lax.fori_loop(0, tb, body, 0, unroll=8)

    @pl.when((i == 0) & (n_used > 0))
    def _():
        start_rows(0)

    @pl.when(i + 1 < n_used)
    def _():
        start_rows(i + 1)

    @pl.when((i == 0) & (n_used > 0))
    def _():
        for c in weight_copies(be_ref[0]):
            c.start(priority=1)

    @pl.when(first_ref[i] == 1)
    def _():
        for c in weight_copies(be_ref[i]):
            c.wait()
        wg_s[...] = wg_f[...].astype(BF16)
        wu_s[...] = wu_f[...].astype(BF16)
        wd_s[...] = wd_f[...].astype(BF16)

        @pl.when(next_ref[i] >= 0)
        def _():
            for c in weight_copies(next_ref[i]):
                c.start(priority=1)

    @pl.when(i < n_used)
    def _():
        wait_rows(i)
        x = x_buf[i % 2].astype(BF16)
        a = jnp.dot(x, wg_s[...], preferred_element_type=F32)
        u = jnp.dot(x, wu_s[...], preferred_element_type=F32)
        y_ref[...] = jnp.dot((_silu(a) * u).astype(BF16), wd_s[...], preferred_element_type=F32)

    @pl.when(i >= n_used)
    def _():
        y_ref[...] = jnp.zeros(y_ref.shape, F32)


def moe_experts(h_all, buf_tok, block_e, n_used, first, next_e, w_gate, w_up, w_down, layer):
    cap = buf_tok.shape[0]
    tb = MOE_BLOCK
    any_spec = pl.BlockSpec(memory_space=pl.ANY)
    grid_spec = pltpu.PrefetchScalarGridSpec(
        num_scalar_prefetch=5,
        grid=(cap // tb,),
        in_specs=[any_spec, any_spec, any_spec, any_spec],
        out_specs=pl.BlockSpec((tb, D_MODEL), lambda i, *_: (i, 0)),
        scratch_shapes=[pltpu.VMEM((2, tb, D_MODEL), F32), pltpu.SemaphoreType.DMA((2,)),
                        pltpu.SemaphoreType.DMA((3,)),
                        pltpu.VMEM((D_MODEL, MOE_DFF), F32), pltpu.VMEM((D_MODEL, MOE_DFF), F32),
                        pltpu.VMEM((MOE_DFF, D_MODEL), F32),
                        pltpu.VMEM((D_MODEL, MOE_DFF), BF16), pltpu.VMEM((D_MODEL, MOE_DFF), BF16),
                        pltpu.VMEM((MOE_DFF, D_MODEL), BF16)],
    )
    return pl.pallas_call(
        functools.partial(_moe_expert_kernel, layer=layer),
        grid_spec=grid_spec,
        out_shape=jax.ShapeDtypeStruct((cap, D_MODEL), F32),
        compiler_params=_params("arbitrary"),
        name="moe_experts",
    )(block_e, n_used, buf_tok, first, next_e, h_all, w_gate, w_up, w_down)


def _route_kernel(lo_ref, eid_ref, gw_ref, rank_ref, cnt_ref, carry_scr, *, n_valid):
    tm = lo_ref.shape[0]
    i = pl.program_id(0)

    @pl.when(i == 0)
    def _():
        carry_scr[...] = jnp.zeros(carry_scr.shape, F32)

    lo = lo_ref[...]
    lane_i = lax.broadcasted_iota(jnp.int32, lo.shape, 1)
    lane = lane_i.astype(F32)
    row = lax.broadcasted_iota(jnp.int32, lo.shape, 0) + i * tm

    def masked_softmax(mask):
        x = jnp.where(mask, lo, NEG_INF)
        e = jnp.where(mask, jnp.exp(x - jnp.max(x, axis=-1, keepdims=True)), 0.0)
        return e / jnp.sum(e, axis=-1, keepdims=True)

    def first_max(p, mask):
        pm = jnp.where(mask, p, -1.0)
        top = jnp.max(pm, axis=-1, keepdims=True)
        idx = jnp.min(jnp.where(pm == top, lane, float(LANES)), axis=-1, keepdims=True)
        return top, idx

    gmask = lane_i < MOE_GROUPS
    p_g, g_idx = first_max(masked_softmax(gmask), gmask)
    e_lane = lane_i - MOE_GROUPS
    emask = (e_lane >= 0) & (e_lane < MOE_EXPERTS) & ((e_lane // MOE_PER_GROUP).astype(F32) == g_idx)
    pe = masked_softmax(emask)
    w0, i0 = first_max(pe, emask)
    w1, i1 = first_max(pe, emask & (lane != i0))
    tw = w0 + w1
    two = lax.broadcasted_iota(jnp.int32, (tm, MOE_TOPK), 1)
    gw_ref[...] = jnp.where(two == 0, p_g * (w0 / tw), p_g * (w1 / tw))
    eid_ref[...] = jnp.where(two == 0, i0, i1).astype(jnp.int32) - MOE_GROUPS

    cnt = jnp.where((row < n_valid) & ((lane == i0) | (lane == i1)), 1.0, 0.0)
    r_id = lax.broadcasted_iota(jnp.int32, (tm, tm), 0)
    c_id = lax.broadcasted_iota(jnp.int32, (tm, tm), 1)
    earlier = (c_id < r_id).astype(BF16)
    before = jnp.dot(earlier, cnt.astype(BF16), preferred_element_type=F32) + carry_scr[...]
    rank0 = jnp.sum(jnp.where(lane == i0, before, 0.0), axis=-1, keepdims=True)
    rank1 = jnp.sum(jnp.where(lane == i1, before, 0.0), axis=-1, keepdims=True)
    rank_ref[...] = jnp.where(two == 0, rank0, rank1).astype(jnp.int32)
    total = carry_scr[...] + jnp.sum(cnt, axis=0, keepdims=True)
    carry_scr[...] = total
    cnt_ref[...] = total


def moe_route(logits, n_valid):
    n_pad = logits.shape[0]
    tm = ROUTE_BLOCK
    pair = lambda dt: jax.ShapeDtypeStruct((n_pad, MOE_TOPK), dt)
    pair_spec = pl.BlockSpec((tm, MOE_TOPK), lambda i: (i, 0))
    return pl.pallas_call(
        functools.partial(_route_kernel, n_valid=n_valid),
        grid=(n_pad // tm,),
        in_specs=[pl.BlockSpec((tm, ROUTER_COLS), lambda i: (i, 0))],
        out_specs=[pair_spec, pair_spec, pair_spec, pl.BlockSpec((1, ROUTER_COLS), lambda i: (0, 0))],
        out_shape=[pair(jnp.int32), pair(F32), pair(jnp.int32), jax.ShapeDtypeStruct((1, ROUTER_COLS), F32)],
        scratch_shapes=[pltpu.VMEM((1, ROUTER_COLS), F32)],
        compiler_params=_params("arbitrary"),
        name="moe_route",
    )(logits)


def _dispatch(e_ids, rank, lane_counts):
    n = e_ids.shape[0]
    tb = MOE_BLOCK
    counts = lane_counts[0, MOE_GROUPS:MOE_GROUPS + MOE_EXPERTS].astype(jnp.int32)
    padded = ((counts + tb - 1) // tb) * tb
    pend = jnp.cumsum(padded)
    pstart = pend - padded
    dest = pstart[e_ids] + rank
    n_blocks = (n * MOE_TOPK + MOE_EXPERTS * (tb - 1) + tb - 1) // tb
    tok = jnp.repeat(jnp.arange(n, dtype=jnp.int32), MOE_TOPK)
    buf_tok = jnp.zeros((n_blocks * tb,), jnp.int32).at[dest.reshape(-1)].set(tok)
    block_e = jnp.minimum(jnp.searchsorted(pend, jnp.arange(n_blocks, dtype=jnp.int32) * tb, side='right'),
                          MOE_EXPERTS - 1).astype(jnp.int32)
    n_used = (pend[-1] // tb).astype(jnp.int32)
    idx = jnp.arange(n_blocks, dtype=jnp.int32)
    first = ((idx < n_used) & ((idx == 0) | (block_e != jnp.roll(block_e, 1)))).astype(jnp.int32)
    after = (pend[block_e] // tb).astype(jnp.int32)
    next_e = jnp.where(after < n_used, block_e[jnp.minimum(after, n_blocks - 1)], -1).astype(jnp.int32)
    return dest, buf_tok, block_e, n_used.reshape(1), first, next_e


def _combine_kernel(slot_ref, x_ref, ga_ref, w_ref, y_hbm, *rest, tok0, final):
    if final:
        g_ref, o_ref, y_buf, sem = rest
    else:
        o_ref, y_buf, sem = rest
    tm = x_ref.shape[0]
    base = (tok0 + pl.program_id(0) * tm) * MOE_TOPK

    def row_copy(r, k):
        return pltpu.make_async_copy(y_hbm.at[pl.ds(slot_ref[base + r * MOE_TOPK + k], 1)],
                                     y_buf.at[k, pl.ds(r, 1)], sem.at[k])

    def start(r, c):
        for k in range(MOE_TOPK):
            row_copy(r, k).start(priority=k % 2)
        return c

    def wait(r, c):
        for k in range(MOE_TOPK):
            row_copy(r, k).wait()
        return c

    lax.fori_loop(0, tm, start, 0, unroll=8)
    lax.fori_loop(0, tm, wait, 0, unroll=8)
    w = w_ref[...]
    out = x_ref[...] + ga_ref[...] * (y_buf[0] * w[:, 0:1] + y_buf[1] * w[:, 1:2])
    if final:
        ms = jnp.mean(out * out, axis=-1, keepdims=True)
        out = (out * lax.rsqrt(ms + RMS_EPS)) * g_ref[...]
    o_ref[...] = out


def moe_combine(x, mods, y, slot_flat, gate_w, tok0, tm, rows_per_batch, final_gain=None):
    m = x.shape[0]
    final = final_gain is not None
    if mods.ndim == 3:
        ga_spec = pl.BlockSpec((None, 1, D_MODEL), lambda i, s: ((i * tm) // rows_per_batch, 0, 5))
    else:
        ga_spec = pl.BlockSpec((tm, D_MODEL), lambda i, s: (i, 5))
    row_spec = pl.BlockSpec((tm, D_MODEL), lambda i, s: (i, 0))
    in_specs = [row_spec, ga_spec,
                pl.BlockSpec((tm, MOE_TOPK), lambda i, s: (tok0 // tm + i, 0)),
                pl.BlockSpec(memory_space=pl.ANY)]
    args = [x, mods, gate_w, y]
    if final:
        in_specs.append(pl.BlockSpec((1, D_MODEL), lambda i, s: (0, 0)))
        args.append(final_gain.reshape(1, D_MODEL))
    grid_spec = pltpu.PrefetchScalarGridSpec(
        num_scalar_prefetch=1,
        grid=(m // tm,),
        in_specs=in_specs,
        out_specs=row_spec,
        scratch_shapes=[pltpu.VMEM((MOE_TOPK, tm, D_MODEL), F32), pltpu.SemaphoreType.DMA((MOE_TOPK,))],
    )
    return pl.pallas_call(
        functools.partial(_combine_kernel, tok0=tok0, final=final),
        grid_spec=grid_spec,
        out_shape=jax.ShapeDtypeStruct((m, D_MODEL), F32),
        compiler_params=_params("arbitrary"),
        name="moe_combine",
    )(slot_flat, *args)


def kernel(x_prompt, x_sample, c_prompt, c_sample, state_gla, cache_k, cache_v, cache_logf, page_table, norm_mix, norm_ffn, norm_final, w_ada, b_ada, gla_w_in, gla_w_gate_up, gla_b_gate, gla_norm, gla_w_out, fox_w_in, fox_b_f, fox_q_norm, fox_k_norm, fox_w_out, moe_w_group, moe_b_group, moe_w_expert, moe_b_expert, moe_w_gate, moe_w_up, moe_w_down):
    bsz, seq, d = x_prompt.shape
    db, ds, _ = x_sample.shape
    depth = w_ada.shape[0]
    mp, msz = bsz * seq, db * ds
    ti_p = min(1024, seq)
    tm_p = min(512, seq)
    te_p = min(256, seq)
    n_tok = mp + msz
    n_route = -(-n_tok // ROUTE_BLOCK) * ROUTE_BLOCK
    assert d == D_MODEL and seq % ti_p == 0 and seq % GLA_CHUNK == 0 and seq % ATTN_BLOCK == 0
    assert ds <= GLA_CHUNK and page_table.shape[1] % PAGES_PER_STEP == 0
    assert page_table.shape[1] % SUFFIX_PAGES_PER_STEP == 0


    c_rows = jnp.concatenate([c_prompt, c_sample], axis=0)
    c_rows = jnp.pad(c_rows, ((0, (-c_rows.shape[0]) % 8), (0, 0)))
    ada = ada_all(c_rows, w_ada, b_ada)

    xp = x_prompt.reshape(mp, d)
    xs = x_sample.reshape(msz, d)
    gla_p, gla_s, kp_l, vp_l, lfp_l, ks_l, vs_l, lfs_l = [], [], [], [], [], [], [], []
    for i in range(depth):
        j = i // 2
        mods_p = ada[i, :bsz].reshape(bsz, 1, 6 * d)
        mods_s = jnp.repeat(ada[i, bsz:bsz + db], ds, axis=0)
        if i % 2 == 0:
            zp, lgp = gla_in_proj(xp, norm_mix[i], mods_p, gla_w_in, j, gla_w_gate_up[j], gla_b_gate[j],
                                  ti_p, seq)
            s0 = jnp.zeros((bsz, GLA_HEADS, GLA_DKH, GLA_DVH), F32)
            op, s_fin = gla_scan(zp.reshape(bsz, seq, GLA_MAIN), lgp.reshape(bsz, seq, GLA_DK), s0)
            xp = out_proj(op.reshape(mp, d), zp, 2, gla_norm[j], gla_w_out, j, xp, mods_p, tm_p, seq)
            gla_p.append(s_fin)

            zs, lgs = gla_in_proj(xs, norm_mix[i], mods_s, gla_w_in, j, gla_w_gate_up[j], gla_b_gate[j],
                                  msz, msz)
            pad = ((0, 0), (0, GLA_CHUNK - ds), (0, 0))
            zs_pad = jnp.pad(zs.reshape(db, ds, GLA_MAIN), pad)
            lgs_pad = jnp.pad(lgs.reshape(db, ds, GLA_DK), pad)
            os_pad, s_new = gla_scan(zs_pad, lgs_pad, state_gla[j])
            xs = out_proj(os_pad[:, :ds].reshape(msz, d), zs, 2, gla_norm[j], gla_w_out, j, xs, mods_s,
                          msz, msz)
            gla_s.append(s_new)
        else:
            qp, kp, vp, gp, lfp = fox_in_proj(xp, norm_mix[i], mods_p, fox_w_in, j, fox_b_f[j],
                                              fox_q_norm[j], fox_k_norm[j], ti_p, seq)
            rows3 = (bsz, seq, FOX_DIM)
            lf3 = lfp.reshape(bsz, seq, FOX_HEADS)
            op = fox_prompt_attn(qp.reshape(rows3), kp.reshape(rows3), vp.reshape(rows3), cumsum_time(lf3))
            xp = out_proj(op.reshape(mp, d), gp, 0, None, fox_w_out, j, xp, mods_p, tm_p, seq)
            kp_l.append(kp.reshape(bsz, seq, FOX_HEADS, FOX_HD))
            vp_l.append(vp.reshape(bsz, seq, FOX_HEADS, FOX_HD))
            lfp_l.append(lf3)

            qs, ks, vs, gs, lfs = fox_in_proj(xs, norm_mix[i], mods_s, fox_w_in, j, fox_b_f[j],
                                              fox_q_norm[j], fox_k_norm[j], msz, msz)
            heads = (db, ds, FOX_HEADS, FOX_HD)
            lfs3 = lfs.reshape(db, ds, FOX_HEADS)
            os_ = fox_paged_attn(qs.reshape(heads), ks.reshape(heads), vs.reshape(heads), lfs3,
                                 cache_k, cache_v, cache_logf, j, page_table)
            xs = out_proj(os_.reshape(msz, d), gs, 0, None, fox_w_out, j, xs, mods_s, msz, msz)
            ks_l.append(ks.reshape(heads))
            vs_l.append(vs.reshape(heads))
            lfs_l.append(lfs3)

        w_router = jnp.pad(jnp.concatenate([moe_w_group[i], moe_w_expert[i]], axis=1),
                           ((0, 0), (0, ROUTER_COLS - MOE_GROUPS - MOE_EXPERTS))).astype(BF16)
        b_router = jnp.pad(jnp.concatenate([moe_b_group[i], moe_b_expert[i]]),
                           (0, ROUTER_COLS - MOE_GROUPS - MOE_EXPERTS)).reshape(1, ROUTER_COLS)
        h_all, lop = moe_pre(xp, norm_ffn[i], mods_p, w_router, b_router, tm_p, seq, n_tok, 0)
        h_all, los = moe_pre(xs, norm_ffn[i], mods_s, w_router, b_router, msz, msz, n_tok, mp, h_all)
        logits = jnp.concatenate([lop, los, jnp.zeros((n_route - n_tok, ROUTER_COLS), F32)], axis=0)
        e_ids, gate_w, rank, lane_counts = moe_route(logits, n_tok)
        slot, buf_tok, block_e, n_used, first, next_e = _dispatch(e_ids[:n_tok], rank[:n_tok], lane_counts)
        y = moe_experts(h_all, buf_tok, block_e, n_used, first, next_e, moe_w_gate, moe_w_up, moe_w_down, i)
        slot_flat = slot.reshape(-1)
        closing = norm_final if i == depth - 1 else None
        xp = moe_combine(xp, mods_p, y, slot_flat, gate_w, 0, te_p, seq, closing)
        xs = moe_combine(xs, mods_s, y, slot_flat, gate_w, mp, msz, msz, closing)

    y_prompt = xp.reshape(bsz, seq, d)
    y_sample = xs.reshape(db, ds, d)
    return (y_prompt, y_sample,
            jnp.stack(kp_l), jnp.stack(vp_l), jnp.stack(lfp_l), jnp.stack(gla_p),
            jnp.stack(ks_l), jnp.stack(vs_l), jnp.stack(lfs_l), jnp.stack(gla_s))
```

```python
import functools

import jax
import jax.numpy as jnp
from jax import lax
from jax.experimental import pallas as pl
from jax.experimental.pallas import tpu as pltpu

F32 = jnp.float32
BF16 = jnp.bfloat16

D_MODEL = 2048
GLA_HEADS = 4
GLA_DK = D_MODEL // 2
GLA_DV = D_MODEL
GLA_DKH = GLA_DK // GLA_HEADS
GLA_DVH = GLA_DV // GLA_HEADS
GLA_GATE_RANK = 16
GLA_GATE_NORM = 16.0
GLA_MAIN = 2 * GLA_DK + 2 * GLA_DV
FOX_HEADS = 16
FOX_HD = D_MODEL // FOX_HEADS
FOX_DIM = FOX_HEADS * FOX_HD
MOE_GROUPS = 4
MOE_PER_GROUP = 8
MOE_EXPERTS = MOE_GROUPS * MOE_PER_GROUP
MOE_TOPK = 2
MOE_DFF = D_MODEL // 4
RMS_EPS = 1e-6
NEG_INF = -1e30

VMEM_LIMIT_BYTES = 52 * 1024 * 1024
LANES = 128

GLA_CHUNK = 128
GLA_SUB = 16
ATTN_BLOCK = 256
PAGES_PER_STEP = 8
SUFFIX_PAGES_PER_STEP = 32
ROUTE_BLOCK = 512
MOE_BLOCK = 256
ROUTER_COLS = 128


def _params(*sem):
    return pltpu.CompilerParams(dimension_semantics=sem, vmem_limit_bytes=VMEM_LIMIT_BYTES)


def _log_sigmoid(x):
    return jnp.minimum(x, 0.0) - jnp.log1p(jnp.exp(-jnp.abs(x)))


def _silu(x):
    return x * jax.nn.sigmoid(x)


def _split3(a):
    hi = a.astype(BF16)
    r1 = a - hi.astype(F32)
    mid = r1.astype(BF16)
    lo = (r1 - mid.astype(F32)).astype(BF16)
    return hi, mid, lo


def _dot_exact_lhs(sel_bf16, x_f32):
    hi, mid, lo = _split3(x_f32)
    d = functools.partial(jnp.dot, preferred_element_type=F32)
    return d(sel_bf16, hi) + d(sel_bf16, mid) + d(sel_bf16, lo)


def _modulate(x, g, shift, scale):
    ms = jnp.mean(x * x, axis=-1, keepdims=True)
    return (x * lax.rsqrt(ms + RMS_EPS)) * g * (1.0 + scale) + shift


def _ada_kernel(c_ref, w_ref, b_ref, o_ref):
    a = _silu(c_ref[...]).astype(BF16)
    o_ref[...] = jnp.dot(a, w_ref[...].astype(BF16), preferred_element_type=F32) + b_ref[...]


def ada_all(c_rows, w_ada, b_ada, tn=1024):
    depth, d, n = w_ada.shape
    rows = c_rows.shape[0]
    return pl.pallas_call(
        _ada_kernel,
        grid=(depth, n // tn),
        in_specs=[
            pl.BlockSpec((rows, d), lambda l, j: (0, 0)),
            pl.BlockSpec((None, d, tn), lambda l, j: (l, 0, j)),
            pl.BlockSpec((None, 1, tn), lambda l, j: (l, 0, j)),
        ],
        out_specs=pl.BlockSpec((None, rows, tn), lambda l, j: (l, 0, j)),
        out_shape=jax.ShapeDtypeStruct((depth, rows, n), F32),
        compiler_params=_params("arbitrary", "arbitrary"),
        name="ada_all",
    )(c_rows, w_ada, b_ada.reshape(depth, 1, n))


def _mod_specs(mods, chunk_ids, tm, rows_per_batch):
    specs = []
    for c in chunk_ids:
        if mods.ndim == 3:
            specs.append(pl.BlockSpec((None, 1, D_MODEL),
                                      lambda i, j, c=c: ((i * tm) // rows_per_batch, 0, c)))
        else:
            specs.append(pl.BlockSpec((tm, D_MODEL), lambda i, j, c=c: (i, c)))
    return specs


def _gla_in_kernel(x_ref, g_ref, sh_ref, sc_ref, w_ref, wgd_ref, wup_ref, bg_ref, z_ref, lg_ref, h_scr):
    @pl.when(pl.program_id(1) == 0)
    def _():
        h = _modulate(x_ref[...], g_ref[...], sh_ref[...], sc_ref[...]).astype(BF16)
        h_scr[...] = h
        gd = jnp.dot(h, wgd_ref[...].astype(BF16), preferred_element_type=F32)
        gate = jnp.dot(gd.astype(BF16), wup_ref[...].astype(BF16), preferred_element_type=F32) + bg_ref[...]
        lg_ref[...] = _log_sigmoid(gate) * (1.0 / GLA_GATE_NORM)

    z_ref[...] = jnp.dot(h_scr[...], w_ref[...].astype(BF16), preferred_element_type=F32)


def gla_in_proj(x, gain, mods, w_in_all, layer, w_gate_up, b_gate, tm, rows_per_batch, tn=512):
    m = x.shape[0]
    w_gd = w_in_all[layer, :, GLA_MAIN:]
    sh_spec, sc_spec = _mod_specs(mods, (0, 1), tm, rows_per_batch)
    return pl.pallas_call(
        _gla_in_kernel,
        grid=(m // tm, GLA_MAIN // tn),
        in_specs=[
            pl.BlockSpec((tm, D_MODEL), lambda i, j: (i, 0)),
            pl.BlockSpec((1, D_MODEL), lambda i, j: (0, 0)),
            sh_spec, sc_spec,
            pl.BlockSpec((None, D_MODEL, tn), lambda i, j: (layer, 0, j)),
            pl.BlockSpec((D_MODEL, GLA_GATE_RANK), lambda i, j: (0, 0)),
            pl.BlockSpec((GLA_GATE_RANK, GLA_DK), lambda i, j: (0, 0)),
            pl.BlockSpec((1, GLA_DK), lambda i, j: (0, 0)),
        ],
        out_specs=[
            pl.BlockSpec((tm, tn), lambda i, j: (i, j)),
            pl.BlockSpec((tm, GLA_DK), lambda i, j: (i, 0)),
        ],
        out_shape=[jax.ShapeDtypeStruct((m, GLA_MAIN), F32), jax.ShapeDtypeStruct((m, GLA_DK), F32)],
        scratch_shapes=[pltpu.VMEM((tm, D_MODEL), BF16)],
        compiler_params=_params("arbitrary", "arbitrary"),
        name="gla_in_proj",
    )(x, gain.reshape(1, D_MODEL), mods, mods, w_in_all, w_gd, w_gate_up, b_gate.reshape(1, GLA_DK))


def _gla_scan_kernel(q_ref, k_ref, v_ref, lg_ref, s0_ref, o_ref, s_ref):
    C, R = GLA_CHUNK, GLA_SUB

    @pl.when(pl.program_id(2) == 0)
    def _():
        s_ref[...] = s0_ref[...]

    q = q_ref[...] * (GLA_DKH ** -0.5)
    k = k_ref[...]
    v = v_ref[...]
    vb = v.astype(BF16)
    row = lax.broadcasted_iota(jnp.int32, (C, C), 0)
    col = lax.broadcasted_iota(jnp.int32, (C, C), 1)
    tri = (row >= col).astype(BF16)
    b = _dot_exact_lhs(tri, lg_ref[...])
    state = s_ref[...]
    inter = jnp.dot((q * jnp.exp(b)).astype(BF16), state.astype(BF16), preferred_element_type=F32)

    t_idx = lax.broadcasted_iota(jnp.int32, (R, 1), 0)
    s_idx = lax.broadcasted_iota(jnp.int32, (R, R), 1)
    for i in range(C // R):
        lo = i * R
        bi = b[lo:lo + R]
        qi = q[lo:lo + R]
        ki = k[lo:lo + R]
        oi = inter[lo:lo + R]
        if i > 0:
            b_ref_row = b[lo - 1:lo]
            qe = (qi * jnp.exp(bi - b_ref_row)).astype(BF16)
            ke = (k[:lo] * jnp.exp(b_ref_row - b[:lo])).astype(BF16)
            a = lax.dot_general(qe, ke, (((1,), (1,)), ((), ())), preferred_element_type=F32)
            oi = oi + jnp.dot(a.astype(BF16), vb[:lo], preferred_element_type=F32)
        a_diag = jnp.zeros((R, R), F32)
        for s in range(R):
            rel = jnp.where(t_idx >= s, bi - bi[s:s + 1], NEG_INF)
            w = jnp.sum(qi * ki[s:s + 1] * jnp.exp(rel), axis=-1, keepdims=True)
            a_diag = jnp.where(s_idx == s, w, a_diag)
        oi = oi + jnp.dot(a_diag.astype(BF16), vb[lo:lo + R], preferred_element_type=F32)
        o_ref[lo:lo + R, :] = oi

    b_t = b.T
    b_last = b_t[:, C - 1:C]
    ke_t = (k.T * jnp.exp(b_last - b_t)).astype(BF16)
    s_ref[...] = jnp.exp(b_last) * state + jnp.dot(ke_t, vb, preferred_element_type=F32)


def gla_scan(z, lg, s0):
    bsz, t, _ = z.shape
    C = GLA_CHUNK
    kq = GLA_DK // GLA_DKH
    return pl.pallas_call(
        _gla_scan_kernel,
        grid=(bsz, GLA_HEADS, t // C),
        in_specs=[
            pl.BlockSpec((None, C, GLA_DKH), lambda b, h, c: (b, c, h)),
            pl.BlockSpec((None, C, GLA_DKH), lambda b, h, c: (b, c, kq + h)),
            pl.BlockSpec((None, C, GLA_DVH), lambda b, h, c: (b, c, (2 * GLA_DK) // GLA_DVH + h)),
            pl.BlockSpec((None, C, GLA_DKH), lambda b, h, c: (b, c, h)),
            pl.BlockSpec((None, None, GLA_DKH, GLA_DVH), lambda b, h, c: (b, h, 0, 0)),
        ],
        out_specs=[
            pl.BlockSpec((None, C, GLA_DVH), lambda b, h, c: (b, c, h)),
            pl.BlockSpec((None, None, GLA_DKH, GLA_DVH), lambda b, h, c: (b, h, 0, 0)),
        ],
        out_shape=[jax.ShapeDtypeStruct((bsz, t, GLA_DV), F32),
                   jax.ShapeDtypeStruct((bsz, GLA_HEADS, GLA_DKH, GLA_DVH), F32)],
        compiler_params=_params("arbitrary", "arbitrary", "arbitrary"),
        name="gla_scan",
    )(z, z, z, lg, s0)


def _gla_out_kernel(o_ref, g_ref, ng_ref, w_ref, x_ref, ga_ref, y_ref, p_scr):
    @pl.when(pl.program_id(1) == 0)
    def _():
        gate = _silu(g_ref[...])
        for h in range(GLA_HEADS):
            sl = slice(h * GLA_DVH, (h + 1) * GLA_DVH)
            o = o_ref[:, sl]
            ms = jnp.mean(o * o, axis=-1, keepdims=True)
            p_scr[:, sl] = ((o * lax.rsqrt(ms + RMS_EPS)) * ng_ref[...] * gate[:, sl]).astype(BF16)

    y_ref[...] = x_ref[...] + ga_ref[...] * jnp.dot(p_scr[...], w_ref[...].astype(BF16),
                                                    preferred_element_type=F32)


def _fox_out_kernel(o_ref, g_ref, w_ref, x_ref, ga_ref, y_ref, p_scr):
    @pl.when(pl.program_id(1) == 0)
    def _():
        p_scr[...] = (o_ref[...] * jax.nn.sigmoid(g_ref[...])).astype(BF16)

    y_ref[...] = x_ref[...] + ga_ref[...] * jnp.dot(p_scr[...], w_ref[...].astype(BF16),
                                                    preferred_element_type=F32)


def out_proj(o, z, g_block, norm_gain, w_out_all, layer, x, mods, tm, rows_per_batch, tn=512):
    m = x.shape[0]
    row_spec = pl.BlockSpec((tm, D_MODEL), lambda i, j: (i, 0))
    in_specs = [row_spec, pl.BlockSpec((tm, D_MODEL), lambda i, j: (i, g_block))]
    args = [o, z]
    if norm_gain is not None:
        in_specs.append(pl.BlockSpec((1, GLA_DVH), lambda i, j: (0, 0)))
        args.append(norm_gain.reshape(1, GLA_DVH))
        body = _gla_out_kernel
    else:
        body = _fox_out_kernel
    in_specs += [pl.BlockSpec((None, D_MODEL, tn), lambda i, j: (layer, 0, j)),
                 pl.BlockSpec((tm, tn), lambda i, j: (i, j))]
    args += [w_out_all, x]
    if mods.ndim == 3:
        ga_spec = pl.BlockSpec((None, 1, tn),
                               lambda i, j: ((i * tm) // rows_per_batch, 0, 2 * (D_MODEL // tn) + j))
    else:
        ga_spec = pl.BlockSpec((tm, tn), lambda i, j: (i, 2 * (D_MODEL // tn) + j))
    in_specs.append(ga_spec)
    args.append(mods)
    return pl.pallas_call(
        body,
        grid=(m // tm, D_MODEL // tn),
        in_specs=in_specs,
        out_specs=pl.BlockSpec((tm, tn), lambda i, j: (i, j)),
        out_shape=jax.ShapeDtypeStruct((m, D_MODEL), F32),
        scratch_shapes=[pltpu.VMEM((tm, D_MODEL), BF16)],
        compiler_params=_params("arbitrary", "arbitrary"),
        name="out_proj",
    )(*args)


def _fox_in_kernel(x_ref, g_ref, sh_ref, sc_ref, w_ref, wf_ref, bf_ref, qn_ref, kn_ref,
                   q_ref, k_ref, v_ref, go_ref, lf_ref, h_scr, *, tn):
    j = pl.program_id(1)
    nt = FOX_DIM // tn

    @pl.when(j == 0)
    def _():
        h = _modulate(x_ref[...], g_ref[...], sh_ref[...], sc_ref[...]).astype(BF16)
        h_scr[...] = h
        f = jnp.dot(h, wf_ref[...].astype(BF16), preferred_element_type=F32)
        lf_ref[...] = _log_sigmoid(f + bf_ref[...])

    acc = jnp.dot(h_scr[...], w_ref[...].astype(BF16), preferred_element_type=F32)

    def head_norm(o_ref, gain):
        for c in range(tn // FOX_HD):
            blk = acc[:, c * FOX_HD:(c + 1) * FOX_HD]
            ms = jnp.mean(blk * blk, axis=-1, keepdims=True)
            o_ref[:, c * FOX_HD:(c + 1) * FOX_HD] = (blk * lax.rsqrt(ms + RMS_EPS)) * gain

    @pl.when(j < nt)
    def _():
        head_norm(q_ref, qn_ref[...])

    @pl.when((j >= nt) & (j < 2 * nt))
    def _():
        head_norm(k_ref, kn_ref[...])

    @pl.when((j >= 2 * nt) & (j < 3 * nt))
    def _():
        v_ref[...] = acc

    @pl.when(j >= 3 * nt)
    def _():
        go_ref[...] = acc


def fox_in_proj(x, gain, mods, w_in_all, layer, b_f, q_norm, k_norm, tm, rows_per_batch, tn=512):
    m = x.shape[0]
    n_main = 4 * FOX_DIM
    nt = FOX_DIM // tn
    w_f = w_in_all[layer, :, n_main:]
    sh_spec, sc_spec = _mod_specs(mods, (0, 1), tm, rows_per_batch)
    col = lambda j, g: jnp.clip(j - g * nt, 0, nt - 1)
    flat_spec = lambda g: pl.BlockSpec((tm, tn), lambda i, j: (i, col(j, g)))
    flat = jax.ShapeDtypeStruct((m, FOX_DIM), F32)
    return pl.pallas_call(
        functools.partial(_fox_in_kernel, tn=tn),
        grid=(m // tm, n_main // tn),
        in_specs=[
            pl.BlockSpec((tm, D_MODEL), lambda i, j: (i, 0)),
            pl.BlockSpec((1, D_MODEL), lambda i, j: (0, 0)),
            sh_spec, sc_spec,
            pl.BlockSpec((None, D_MODEL, tn), lambda i, j: (layer, 0, j)),
            pl.BlockSpec((D_MODEL, FOX_HEADS), lambda i, j: (0, 0)),
            pl.BlockSpec((1, FOX_HEADS), lambda i, j: (0, 0)),
            pl.BlockSpec((1, FOX_HD), lambda i, j: (0, 0)),
            pl.BlockSpec((1, FOX_HD), lambda i, j: (0, 0)),
        ],
        out_specs=[flat_spec(0), flat_spec(1), flat_spec(2), flat_spec(3),
                   pl.BlockSpec((tm, FOX_HEADS), lambda i, j: (i, 0))],
        out_shape=[flat, flat, flat, flat, jax.ShapeDtypeStruct((m, FOX_HEADS), F32)],
        scratch_shapes=[pltpu.VMEM((tm, D_MODEL), BF16)],
        compiler_params=_params("arbitrary", "arbitrary"),
        name="fox_in_proj",
    )(x, gain.reshape(1, D_MODEL), mods, mods, w_in_all, w_f, b_f.reshape(1, FOX_HEADS),
      q_norm.reshape(1, FOX_HD), k_norm.reshape(1, FOX_HD))


def _cumsum_kernel(x_ref, o_ref):
    t = x_ref.shape[0]
    blk = LANES
    row = lax.broadcasted_iota(jnp.int32, (blk, blk), 0)
    col = lax.broadcasted_iota(jnp.int32, (blk, blk), 1)
    tri = (row >= col).astype(BF16)
    carry = jnp.zeros((1, x_ref.shape[1]), F32)
    for i in range(t // blk):
        c = _dot_exact_lhs(tri, x_ref[i * blk:(i + 1) * blk, :]) + carry
        o_ref[i * blk:(i + 1) * blk, :] = c
        carry = c[blk - 1:blk]


def cumsum_time(x):
    bsz, t, h = x.shape
    return pl.pallas_call(
        _cumsum_kernel,
        grid=(bsz,),
        in_specs=[pl.BlockSpec((None, t, h), lambda b: (b, 0, 0))],
        out_specs=pl.BlockSpec((None, t, h), lambda b: (b, 0, 0)),
        out_shape=jax.ShapeDtypeStruct((bsz, t, h), F32),
        compiler_params=_params("arbitrary"),
        name="cumsum_time",
    )(x)


def _fox_attn_kernel(q_ref, k_ref, v_ref, cc_ref, cr_ref, o_ref):
    blk = ATTN_BLOCK
    t = q_ref.shape[0]
    nt_dims = (((1,), (1,)), ((), ()))
    kb = k_ref[...].astype(BF16)
    vb = v_ref[...].astype(BF16)
    row = lax.broadcasted_iota(jnp.int32, (blk, blk), 0)
    col = lax.broadcasted_iota(jnp.int32, (blk, blk), 1)
    for i in range(t // blk):
        lo = i * blk
        q = (q_ref[lo:lo + blk, :] * (FOX_HD ** -0.5)).astype(BF16)
        cq = cc_ref[lo:lo + blk, :]
        s_d = lax.dot_general(q, kb[lo:lo + blk], nt_dims, preferred_element_type=F32)
        s_d = jnp.where(col <= row, s_d + cq - cr_ref[:, lo:lo + blk], NEG_INF)
        m = jnp.max(s_d, axis=-1, keepdims=True)
        if i > 0:
            s_p = lax.dot_general(q, kb[:lo], nt_dims, preferred_element_type=F32) + cq - cr_ref[:, :lo]
            m = jnp.maximum(m, jnp.max(s_p, axis=-1, keepdims=True))
        p_d = jnp.exp(s_d - m)
        l = jnp.sum(p_d, axis=-1, keepdims=True)
        acc = jnp.dot(p_d.astype(BF16), vb[lo:lo + blk], preferred_element_type=F32)
        if i > 0:
            p_p = jnp.exp(s_p - m)
            l = l + jnp.sum(p_p, axis=-1, keepdims=True)
            acc = acc + jnp.dot(p_p.astype(BF16), vb[:lo], preferred_element_type=F32)
        o_ref[lo:lo + blk, :] = acc / l


def fox_prompt_attn(q, k, v, cum):
    bsz, t, _ = q.shape
    cum_h = jnp.transpose(cum, (0, 2, 1))
    cum_col = cum_h.reshape(bsz, FOX_HEADS, t, 1)
    cum_row = cum_h.reshape(bsz, FOX_HEADS, 1, t)
    head_spec = pl.BlockSpec((None, t, FOX_HD), lambda b, h: (b, 0, h))
    return pl.pallas_call(
        _fox_attn_kernel,
        grid=(bsz, FOX_HEADS),
        in_specs=[
            head_spec, head_spec, head_spec,
            pl.BlockSpec((None, None, t, 1), lambda b, h: (b, h, 0, 0)),
            pl.BlockSpec((None, None, 1, t), lambda b, h: (b, h, 0, 0)),
        ],
        out_specs=head_spec,
        out_shape=jax.ShapeDtypeStruct((bsz, t, FOX_DIM), F32),
        compiler_params=_params("arbitrary", "arbitrary"),
        name="fox_prompt_attn",
    )(q, k, v, cum_col, cum_row)


def _fox_suffix_kernel(pt_ref, *rest):
    G = SUFFIX_PAGES_PER_STEP
    lf_refs = rest[:G]
    o_ref, carry_scr = rest[G:]
    width = lf_refs[0].shape[1]

    @pl.when(pl.program_id(1) == 0)
    def _():
        carry_scr[...] = jnp.zeros(carry_scr.shape, F32)

    lf = jnp.concatenate([r[...] for r in lf_refs], axis=0)
    lane = lax.broadcasted_iota(jnp.int32, (G, width), 1)
    inc = lf
    s = FOX_HEADS
    while s < width:
        inc = inc + jnp.where(lane + s < width, pltpu.roll(inc, width - s, axis=1), 0.0)
        s *= 2
    tot = jnp.where(lane < FOX_HEADS, inc, 0.0)
    s = FOX_HEADS
    while s < width:
        tot = tot + pltpu.roll(tot, s, axis=1)
        s *= 2
    later = inc - lf
    carry = carry_scr[...]
    for g in range(G):
        o_ref[G - 1 - g] = later[g:g + 1] + carry
        carry = carry + tot[g:g + 1]
    carry_scr[...] = carry


def fox_suffix(cache_logf, layer, page_table):
    db, n_pages = page_table.shape
    n_fox, n_pool, page, _ = cache_logf.shape
    width = page * FOX_HEADS
    G = SUFFIX_PAGES_PER_STEP
    steps = n_pages // G
    lf_flat = cache_logf.reshape(n_fox, n_pool, 1, width)
    lf_specs = [pl.BlockSpec((None, None, 1, width),
                             lambda b, t, pt, g=g: (layer, pt[b, n_pages - 1 - (t * G + g)], 0, 0))
                for g in range(G)]
    grid_spec = pltpu.PrefetchScalarGridSpec(
        num_scalar_prefetch=1,
        grid=(db, steps),
        in_specs=lf_specs,
        out_specs=pl.BlockSpec((None, G, 1, width), lambda b, t, pt: (b, steps - 1 - t, 0, 0)),
        scratch_shapes=[pltpu.VMEM((1, width), F32)],
    )
    return pl.pallas_call(
        _fox_suffix_kernel,
        grid_spec=grid_spec,
        out_shape=jax.ShapeDtypeStruct((db, n_pages, 1, width), F32),
        compiler_params=_params("arbitrary", "arbitrary"),
        name="fox_suffix",
    )(page_table, *([lf_flat] * G))


def _fox_paged_kernel(pt_ref, q_ref, ct_ref, kn_ref, vn_ref, bn_ref, suf_ref, *rest):
    G = PAGES_PER_STEP
    k_refs = rest[:G]
    v_refs = rest[G:2 * G]
    o_ref = rest[2 * G]
    m_scr, l_scr, acc_scr = rest[2 * G + 1:]
    t = pl.program_id(1)
    page = k_refs[0].shape[0]
    rows = q_ref.shape[0]
    nt_dims = (((1,), (1,)), ((), ()))

    @pl.when(t == 0)
    def _():
        m_scr[...] = jnp.full(m_scr.shape, NEG_INF, F32)
        l_scr[...] = jnp.zeros(l_scr.shape, F32)
        acc_scr[...] = jnp.zeros(acc_scr.shape, F32)

    q = q_ref[...].astype(BF16)
    r_id = lax.broadcasted_iota(jnp.int32, (rows, LANES), 0)
    c_id = lax.broadcasted_iota(jnp.int32, (rows, LANES), 1)
    head_bias = jnp.where(r_id % FOX_HEADS == c_id % FOX_HEADS, 0.0, NEG_INF)
    head_bias = jnp.concatenate([head_bias] * (page * FOX_HEADS // LANES), axis=1) + ct_ref[...]

    def online(carry, s_list, v_list):
        m, l, acc = carry
        m_new = m
        for s in s_list:
            m_new = jnp.maximum(m_new, jnp.max(s, axis=-1, keepdims=True))
        alpha = jnp.exp(m - m_new)
        l = alpha * l
        acc = alpha * acc
        for s, v2 in zip(s_list, v_list):
            p = jnp.exp(s - m_new)
            l = l + jnp.sum(p, axis=-1, keepdims=True)
            acc = acc + jnp.dot(p.astype(BF16), v2, preferred_element_type=F32)
        return m_new, l, acc

    s_list, v_list = [], []
    for g in range(G):
        k2 = k_refs[g][...].reshape(page * FOX_HEADS, FOX_HD).astype(BF16)
        v_list.append(v_refs[g][...].reshape(page * FOX_HEADS, FOX_HD).astype(BF16))
        s_list.append(lax.dot_general(q, k2, nt_dims, preferred_element_type=F32) + (head_bias + suf_ref[g]))
    carry = online((m_scr[...], l_scr[...], acc_scr[...]), s_list, v_list)
    m_scr[...], l_scr[...], acc_scr[...] = carry

    @pl.when(t == pl.num_programs(1) - 1)
    def _():
        s = lax.dot_general(q, kn_ref[...].astype(BF16), nt_dims, preferred_element_type=F32) + bn_ref[...]
        _, l, acc = online(carry, [s], [vn_ref[...].astype(BF16)])
        o_ref[...] = acc / l


def fox_paged_attn(q, k_new, v_new, logf_new, cache_k, cache_v, cache_logf, layer, page_table):
    db, s_len, _, _ = q.shape
    n_pages = page_table.shape[1]
    page = cache_k.shape[2]
    G = PAGES_PER_STEP
    rows = s_len * FOX_HEADS
    new_cols = LANES
    assert rows % 8 == 0 and rows <= new_cols and (page * FOX_HEADS) % LANES == 0

    suf = fox_suffix(cache_logf, layer, page_table)

    q2 = (q * (FOX_HD ** -0.5)).reshape(db, rows, FOX_HD)
    cum = jnp.cumsum(logf_new, axis=1)
    ct = cum.reshape(db, rows, 1)
    same_head = jnp.arange(FOX_HEADS)[:, None] == jnp.arange(FOX_HEADS)[None, :]
    causal = jnp.arange(s_len)[None, :] <= jnp.arange(s_len)[:, None]
    ok = causal[:, None, :, None] & same_head[None, :, None, :]
    bn = jnp.where(ok[None], cum[:, :, :, None, None] - cum[:, None, None, :, :], NEG_INF)
    bn = jnp.pad(bn.reshape(db, rows, rows), ((0, 0), (0, 0), (0, new_cols - rows)), constant_values=NEG_INF)
    kn = jnp.pad(k_new.reshape(db, rows, FOX_HD), ((0, 0), (0, new_cols - rows), (0, 0)))
    vn = jnp.pad(v_new.reshape(db, rows, FOX_HD), ((0, 0), (0, new_cols - rows), (0, 0)))

    kv_specs = [pl.BlockSpec((None, None, page, FOX_HEADS, FOX_HD),
                             lambda b, t, pt, g=g: (layer, pt[b, t * G + g], 0, 0, 0)) for g in range(G)]
    grid_spec = pltpu.PrefetchScalarGridSpec(
        num_scalar_prefetch=1,
        grid=(db, n_pages // G),
        in_specs=[
            pl.BlockSpec((None, rows, FOX_HD), lambda b, t, pt: (b, 0, 0)),
            pl.BlockSpec((None, rows, 1), lambda b, t, pt: (b, 0, 0)),
            pl.BlockSpec((None, new_cols, FOX_HD), lambda b, t, pt: (b, 0, 0)),
            pl.BlockSpec((None, new_cols, FOX_HD), lambda b, t, pt: (b, 0, 0)),
            pl.BlockSpec((None, rows, new_cols), lambda b, t, pt: (b, 0, 0)),
            pl.BlockSpec((None, G, 1, page * FOX_HEADS), lambda b, t, pt: (b, t, 0, 0)),
        ] + kv_specs + kv_specs,
        out_specs=pl.BlockSpec((None, rows, FOX_HD), lambda b, t, pt: (b, 0, 0)),
        scratch_shapes=[pltpu.VMEM((rows, 1), F32), pltpu.VMEM((rows, 1), F32),
                        pltpu.VMEM((rows, FOX_HD), F32)],
    )
    o = pl.pallas_call(
        _fox_paged_kernel,
        grid_spec=grid_spec,
        out_shape=jax.ShapeDtypeStruct((db, rows, FOX_HD), F32),
        compiler_params=_params("arbitrary", "arbitrary"),
        name="fox_paged_attn",
    )(page_table, q2, ct, kn, vn, bn, suf, *([cache_k] * G), *([cache_v] * G))
    return o.reshape(db, s_len, FOX_HEADS, FOX_HD)


def _moe_pre_kernel(x_ref, g_ref, sh_ref, sc_ref, wr_ref, br_ref, *rest):
    h_ref, lo_ref = rest[-2:]
    h = _modulate(x_ref[...], g_ref[...], sh_ref[...], sc_ref[...])
    h_ref[...] = h
    lo_ref[...] = jnp.dot(h.astype(BF16), wr_ref[...], preferred_element_type=F32) + br_ref[...]


def moe_pre(x, gain, mods, w_router, b_router, tm, rows_per_batch, n_tok, tok0, h_prev=None):
    m = x.shape[0]
    tiles = m // tm
    prev = () if h_prev is None else (h_prev,)
    extra = 1 if (h_prev is None and n_tok > m) else 0
    assert n_tok - m <= tm or not extra
    src = lambda i: jnp.minimum(i, tiles - 1)
    if mods.ndim == 3:
        mod_spec = lambda c: pl.BlockSpec((None, 1, D_MODEL),
                                          lambda i, j: ((src(i) * tm) // rows_per_batch, 0, c))
    else:
        mod_spec = lambda c: pl.BlockSpec((tm, D_MODEL), lambda i, j: (src(i), c))
    return pl.pallas_call(
        _moe_pre_kernel,
        grid=(tiles + extra, 1),
        in_specs=[
            pl.BlockSpec((tm, D_MODEL), lambda i, j: (src(i), 0)),
            pl.BlockSpec((1, D_MODEL), lambda i, j: (0, 0)),
            mod_spec(3), mod_spec(4),
            pl.BlockSpec((D_MODEL, ROUTER_COLS), lambda i, j: (0, 0)),
            pl.BlockSpec((1, ROUTER_COLS), lambda i, j: (0, 0)),
        ] + [pl.BlockSpec(memory_space=pl.ANY)] * len(prev),
        out_specs=[
            pl.BlockSpec((tm, D_MODEL), lambda i, j: (tok0 // tm + i, 0)),
            pl.BlockSpec((tm, ROUTER_COLS), lambda i, j: (src(i), 0)),
        ],
        out_shape=[jax.ShapeDtypeStruct((n_tok, D_MODEL), F32), jax.ShapeDtypeStruct((m, ROUTER_COLS), F32)],
        input_output_aliases={6: 0} if prev else {},
        compiler_params=_params("arbitrary", "arbitrary"),
        name="moe_pre",
    )(x, gain.reshape(1, D_MODEL), mods, mods, w_router, b_router, *prev)


def _moe_expert_kernel(be_ref, nu_ref, tok_ref, first_ref, next_ref, h_hbm, wg_hbm, wu_hbm, wd_hbm, y_ref,
                       x_buf, sem, w_sem, wg_f, wu_f, wd_f, wg_s, wu_s, wd_s, *, layer):
    i = pl.program_id(0)
    tb = x_buf.shape[1]
    n_used = nu_ref[0]

    def weight_copies(e):
        return (pltpu.make_async_copy(wg_hbm.at[layer, e], wg_f, w_sem.at[0]),
                pltpu.make_async_copy(wu_hbm.at[layer, e], wu_f, w_sem.at[1]),
                pltpu.make_async_copy(wd_hbm.at[layer, e], wd_f, w_sem.at[2]))

    def row_copy(blk, r, half):
        return pltpu.make_async_copy(h_hbm.at[pl.ds(tok_ref[blk * tb + r], 1)],
                                     x_buf.at[half, pl.ds(r, 1)], sem.at[half])

    def start_rows(blk):
        def body(r, c):
            row_copy(blk, r, blk % 2).start()
            return c
        lax.fori_loop(0, tb, body, 0, unroll=8)

    def wait_rows(blk):
        def body(r, c):
            row_copy(blk, r, blk % 2).wait()
            return c
        lax.fori_loop(0, tb, body, 0, unroll=8)

    @pl.when((i == 0) & (n_used > 0))
    def _():
        start_rows(0)

    @pl.when(i + 1 < n_used)
    def _():
        start_rows(i + 1)

    @pl.when((i == 0) & (n_used > 0))
    def _():
        for c in weight_copies(be_ref[0]):
            c.start(priority=1)

    @pl.when(first_ref[i] == 1)
    def _():
        for c in weight_copies(be_ref[i]):
            c.wait()
        wg_s[...] = wg_f[...].astype(BF16)
        wu_s[...] = wu_f[...].astype(BF16)
        wd_s[...] = wd_f[...].astype(BF16)

        @pl.when(next_ref[i] >= 0)
        def _():
            for c in weight_copies(next_ref[i]):
                c.start(priority=1)

    @pl.when(i < n_used)
    def _():
        wait_rows(i)
        x = x_buf[i % 2].astype(BF16)
        a = jnp.dot(x, wg_s[...], preferred_element_type=F32)
        u = jnp.dot(x, wu_s[...], preferred_element_type=F32)
        y_ref[...] = jnp.dot((_silu(a) * u).astype(BF16), wd_s[...], preferred_element_type=F32)

    @pl.when(i >= n_used)
    def _():
        y_ref[...] = jnp.zeros(y_ref.shape, F32)


def moe_experts(h_all, buf_tok, block_e, n_used, first, next_e, w_gate, w_up, w_down, layer):
    cap = buf_tok.shape[0]
    tb = MOE_BLOCK
    any_spec = pl.BlockSpec(memory_space=pl.ANY)
    grid_spec = pltpu.PrefetchScalarGridSpec(
        num_scalar_prefetch=5,
        grid=(cap // tb,),
        in_specs=[any_spec, any_spec, any_spec, any_spec],
        out_specs=pl.BlockSpec((tb, D_MODEL), lambda i, *_: (i, 0)),
        scratch_shapes=[pltpu.VMEM((2, tb, D_MODEL), F32), pltpu.SemaphoreType.DMA((2,)),
                        pltpu.SemaphoreType.DMA((3,)),
                        pltpu.VMEM((D_MODEL, MOE_DFF), F32), pltpu.VMEM((D_MODEL, MOE_DFF), F32),
                        pltpu.VMEM((MOE_DFF, D_MODEL), F32),
                        pltpu.VMEM((D_MODEL, MOE_DFF), BF16), pltpu.VMEM((D_MODEL, MOE_DFF), BF16),
                        pltpu.VMEM((MOE_DFF, D_MODEL), BF16)],
    )
    return pl.pallas_call(
        functools.partial(_moe_expert_kernel, layer=layer),
        grid_spec=grid_spec,
        out_shape=jax.ShapeDtypeStruct((cap, D_MODEL), F32),
        compiler_params=_params("arbitrary"),
        name="moe_experts",
    )(block_e, n_used, buf_tok, first, next_e, h_all, w_gate, w_up, w_down)


def _route_kernel(lo_ref, eid_ref, gw_ref, rank_ref, cnt_ref, carry_scr, *, n_valid):
    tm = lo_ref.shape[0]
    i = pl.program_id(0)

    @pl.when(i == 0)
    def _():
        carry_scr[...] = jnp.zeros(carry_scr.shape, F32)

    lo = lo_ref[...]
    lane_i = lax.broadcasted_iota(jnp.int32, lo.shape, 1)
    lane = lane_i.astype(F32)
    row = lax.broadcasted_iota(jnp.int32, lo.shape, 0) + i * tm

    def masked_softmax(mask):
        x = jnp.where(mask, lo, NEG_INF)
        e = jnp.where(mask, jnp.exp(x - jnp.max(x, axis=-1, keepdims=True)), 0.0)
        return e / jnp.sum(e, axis=-1, keepdims=True)

    def first_max(p, mask):
        pm = jnp.where(mask, p, -1.0)
        top = jnp.max(pm, axis=-1, keepdims=True)
        idx = jnp.min(jnp.where(pm == top, lane, float(LANES)), axis=-1, keepdims=True)
        return top, idx

    gmask = lane_i < MOE_GROUPS
    p_g, g_idx = first_max(masked_softmax(gmask), gmask)
    e_lane = lane_i - MOE_GROUPS
    emask = (e_lane >= 0) & (e_lane < MOE_EXPERTS) & ((e_lane // MOE_PER_GROUP).astype(F32) == g_idx)
    pe = masked_softmax(emask)
    w0, i0 = first_max(pe, emask)
    w1, i1 = first_max(pe, emask & (lane != i0))
    tw = w0 + w1
    two = lax.broadcasted_iota(jnp.int32, (tm, MOE_TOPK), 1)
    gw_ref[...] = jnp.where(two == 0, p_g * (w0 / tw), p_g * (w1 / tw))
    eid_ref[...] = jnp.where(two == 0, i0, i1).astype(jnp.int32) - MOE_GROUPS

    cnt = jnp.where((row < n_valid) & ((lane == i0) | (lane == i1)), 1.0, 0.0)
    r_id = lax.broadcasted_iota(jnp.int32, (tm, tm), 0)
    c_id = lax.broadcasted_iota(jnp.int32, (tm, tm), 1)
    earlier = (c_id < r_id).astype(BF16)
    before = jnp.dot(earlier, cnt.astype(BF16), preferred_element_type=F32) + carry_scr[...]
    rank0 = jnp.sum(jnp.where(lane == i0, before, 0.0), axis=-1, keepdims=True)
    rank1 = jnp.sum(jnp.where(lane == i1, before, 0.0), axis=-1, keepdims=True)
    rank_ref[...] = jnp.where(two == 0, rank0, rank1).astype(jnp.int32)
    total = carry_scr[...] + jnp.sum(cnt, axis=0, keepdims=True)
    carry_scr[...] = total
    cnt_ref[...] = total


def moe_route(logits, n_valid):
    n_pad = logits.shape[0]
    tm = ROUTE_BLOCK
    pair = lambda dt: jax.ShapeDtypeStruct((n_pad, MOE_TOPK), dt)
    pair_spec = pl.BlockSpec((tm, MOE_TOPK), lambda i: (i, 0))
    return pl.pallas_call(
        functools.partial(_route_kernel, n_valid=n_valid),
        grid=(n_pad // tm,),
        in_specs=[pl.BlockSpec((tm, ROUTER_COLS), lambda i: (i, 0))],
        out_specs=[pair_spec, pair_spec, pair_spec, pl.BlockSpec((1, ROUTER_COLS), lambda i: (0, 0))],
        out_shape=[pair(jnp.int32), pair(F32), pair(jnp.int32), jax.ShapeDtypeStruct((1, ROUTER_COLS), F32)],
        scratch_shapes=[pltpu.VMEM((1, ROUTER_COLS), F32)],
        compiler_params=_params("arbitrary"),
        name="moe_route",
    )(logits)


def _dispatch(e_ids, rank, lane_counts):
    n = e_ids.shape[0]
    tb = MOE_BLOCK
    counts = lane_counts[0, MOE_GROUPS:MOE_GROUPS + MOE_EXPERTS].astype(jnp.int32)
    padded = ((counts + tb - 1) // tb) * tb
    pend = jnp.cumsum(padded)
    pstart = pend - padded
    dest = pstart[e_ids] + rank
    n_blocks = (n * MOE_TOPK + MOE_EXPERTS * (tb - 1) + tb - 1) // tb
    tok = jnp.repeat(jnp.arange(n, dtype=jnp.int32), MOE_TOPK)
    buf_tok = jnp.zeros((n_blocks * tb,), jnp.int32).at[dest.reshape(-1)].set(tok)
    block_e = jnp.minimum(jnp.searchsorted(pend, jnp.arange(n_blocks, dtype=jnp.int32) * tb, side='right'),
                          MOE_EXPERTS - 1).astype(jnp.int32)
    n_used = (pend[-1] // tb).astype(jnp.int32)
    idx = jnp.arange(n_blocks, dtype=jnp.int32)
    first = ((idx < n_used) & ((idx == 0) | (block_e != jnp.roll(block_e, 1)))).astype(jnp.int32)
    after = (pend[block_e] // tb).astype(jnp.int32)
    next_e = jnp.where(after < n_used, block_e[jnp.minimum(after, n_blocks - 1)], -1).astype(jnp.int32)
    return dest, buf_tok, block_e, n_used.reshape(1), first, next_e


def _combine_kernel(slot_ref, x_ref, ga_ref, w_ref, y_hbm, *rest, tok0, final):
    if final:
        g_ref, o_ref, y_buf, sem = rest
    else:
        o_ref, y_buf, sem = rest
    tm = x_ref.shape[0]
    base = (tok0 + pl.program_id(0) * tm) * MOE_TOPK

    def row_copy(r, k):
        return pltpu.make_async_copy(y_hbm.at[pl.ds(slot_ref[base + r * MOE_TOPK + k], 1)],
                                     y_buf.at[k, pl.ds(r, 1)], sem.at[k])

    def start(r, c):
        for k in range(MOE_TOPK):
            row_copy(r, k).start(priority=k % 2)
        return c

    def wait(r, c):
        for k in range(MOE_TOPK):
            row_copy(r, k).wait()
        return c

    lax.fori_loop(0, tm, start, 0, unroll=8)
    lax.fori_loop(0, tm, wait, 0, unroll=8)
    w = w_ref[...]
    out = x_ref[...] + ga_ref[...] * (y_buf[0] * w[:, 0:1] + y_buf[1] * w[:, 1:2])
    if final:
        ms = jnp.mean(out * out, axis=-1, keepdims=True)
        out = (out * lax.rsqrt(ms + RMS_EPS)) * g_ref[...]
    o_ref[...] = out


def moe_combine(x, mods, y, slot_flat, gate_w, tok0, tm, rows_per_batch, final_gain=None):
    m = x.shape[0]
    final = final_gain is not None
    if mods.ndim == 3:
        ga_spec = pl.BlockSpec((None, 1, D_MODEL), lambda i, s: ((i * tm) // rows_per_batch, 0, 5))
    else:
        ga_spec = pl.BlockSpec((tm, D_MODEL), lambda i, s: (i, 5))
    row_spec = pl.BlockSpec((tm, D_MODEL), lambda i, s: (i, 0))
    in_specs = [row_spec, ga_spec,
                pl.BlockSpec((tm, MOE_TOPK), lambda i, s: (tok0 // tm + i, 0)),
                pl.BlockSpec(memory_space=pl.ANY)]
    args = [x, mods, gate_w, y]
    if final:
        in_specs.append(pl.BlockSpec((1, D_MODEL), lambda i, s: (0, 0)))
        args.append(final_gain.reshape(1, D_MODEL))
    grid_spec = pltpu.PrefetchScalarGridSpec(
        num_scalar_prefetch=1,
        grid=(m // tm,),
        in_specs=in_specs,
        out_specs=row_spec,
        scratch_shapes=[pltpu.VMEM((MOE_TOPK, tm, D_MODEL), F32), pltpu.SemaphoreType.DMA((MOE_TOPK,))],
    )
    return pl.pallas_call(
        functools.partial(_combine_kernel, tok0=tok0, final=final),
        grid_spec=grid_spec,
        out_shape=jax.ShapeDtypeStruct((m, D_MODEL), F32),
        compiler_params=_params("arbitrary"),
        name="moe_combine",
    )(slot_flat, *args)


def kernel(x_prompt, x_sample, c_prompt, c_sample, state_gla, cache_k, cache_v, cache_logf, page_table, norm_mix, norm_ffn, norm_final, w_ada, b_ada, gla_w_in, gla_w_gate_up, gla_b_gate, gla_norm, gla_w_out, fox_w_in, fox_b_f, fox_q_norm, fox_k_norm, fox_w_out, moe_w_group, moe_b_group, moe_w_expert, moe_b_expert, moe_w_gate, moe_w_up, moe_w_down):
    bsz, seq, d = x_prompt.shape
    db, ds, _ = x_sample.shape
    depth = w_ada.shape[0]
    mp, msz = bsz * seq, db * ds
    ti_p = min(1024, seq)
    tm_p = min(512, seq)
    te_p = min(256, seq)
    n_tok = mp + msz
    n_route = -(-n_tok // ROUTE_BLOCK) * ROUTE_BLOCK
    assert d == D_MODEL and seq % ti_p == 0 and seq % GLA_CHUNK == 0 and seq % ATTN_BLOCK == 0
    assert ds <= GLA_CHUNK and page_table.shape[1] % PAGES_PER_STEP == 0
    assert page_table.shape[1] % SUFFIX_PAGES_PER_STEP == 0


    c_rows = jnp.concatenate([c_prompt, c_sample], axis=0)
    c_rows = jnp.pad(c_rows, ((0, (-c_rows.shape[0]) % 8), (0, 0)))
    ada = ada_all(c_rows, w_ada, b_ada)

    xp = x_prompt.reshape(mp, d)
    xs = x_sample.reshape(msz, d)
    gla_p, gla_s, kp_l, vp_l, lfp_l, ks_l, vs_l, lfs_l = [], [], [], [], [], [], [], []
    for i in range(depth):
        j = i // 2
        mods_p = ada[i, :bsz].reshape(bsz, 1, 6 * d)
        mods_s = jnp.repeat(ada[i, bsz:bsz + db], ds, axis=0)
        if i % 2 == 0:
            zp, lgp = gla_in_proj(xp, norm_mix[i], mods_p, gla_w_in, j, gla_w_gate_up[j], gla_b_gate[j],
                                  ti_p, seq)
            s0 = jnp.zeros((bsz, GLA_HEADS, GLA_DKH, GLA_DVH), F32)
            op, s_fin = gla_scan(zp.reshape(bsz, seq, GLA_MAIN), lgp.reshape(bsz, seq, GLA_DK), s0)
            xp = out_proj(op.reshape(mp, d), zp, 2, gla_norm[j], gla_w_out, j, xp, mods_p, tm_p, seq)
            gla_p.append(s_fin)

            zs, lgs = gla_in_proj(xs, norm_mix[i], mods_s, gla_w_in, j, gla_w_gate_up[j], gla_b_gate[j],
                                  msz, msz)
            pad = ((0, 0), (0, GLA_CHUNK - ds), (0, 0))
            zs_pad = jnp.pad(zs.reshape(db, ds, GLA_MAIN), pad)
            lgs_pad = jnp.pad(lgs.reshape(db, ds, GLA_DK), pad)
            os_pad, s_new = gla_scan(zs_pad, lgs_pad, state_gla[j])
            xs = out_proj(os_pad[:, :ds].reshape(msz, d), zs, 2, gla_norm[j], gla_w_out, j, xs, mods_s,
                          msz, msz)
            gla_s.append(s_new)
        else:
            qp, kp, vp, gp, lfp = fox_in_proj(xp, norm_mix[i], mods_p, fox_w_in, j, fox_b_f[j],
                                              fox_q_norm[j], fox_k_norm[j], ti_p, seq)
            rows3 = (bsz, seq, FOX_DIM)
            lf3 = lfp.reshape(bsz, seq, FOX_HEADS)
            op = fox_prompt_attn(qp.reshape(rows3), kp.reshape(rows3), vp.reshape(rows3), cumsum_time(lf3))
            xp = out_proj(op.reshape(mp, d), gp, 0, None, fox_w_out, j, xp, mods_p, tm_p, seq)
            kp_l.append(kp.reshape(bsz, seq, FOX_HEADS, FOX_HD))
            vp_l.append(vp.reshape(bsz, seq, FOX_HEADS, FOX_HD))
            lfp_l.append(lf3)

            qs, ks, vs, gs, lfs = fox_in_proj(xs, norm_mix[i], mods_s, fox_w_in, j, fox_b_f[j],
                                              fox_q_norm[j], fox_k_norm[j], msz, msz)
            heads = (db, ds, FOX_HEADS, FOX_HD)
            lfs3 = lfs.reshape(db, ds, FOX_HEADS)
            os_ = fox_paged_attn(qs.reshape(heads), ks.reshape(heads), vs.reshape(heads), lfs3,
                                 cache_k, cache_v, cache_logf, j, page_table)
            xs = out_proj(os_.reshape(msz, d), gs, 0, None, fox_w_out, j, xs, mods_s, msz, msz)
            ks_l.append(ks.reshape(heads))
            vs_l.append(vs.reshape(heads))
            lfs_l.append(lfs3)

        w_router = jnp.pad(jnp.concatenate([moe_w_group[i], moe_w_expert[i]], axis=1),
                           ((0, 0), (0, ROUTER_COLS - MOE_GROUPS - MOE_EXPERTS))).astype(BF16)
        b_router = jnp.pad(jnp.concatenate([moe_b_group[i], moe_b_expert[i]]),
                           (0, ROUTER_COLS - MOE_GROUPS - MOE_EXPERTS)).reshape(1, ROUTER_COLS)
        h_all, lop = moe_pre(xp, norm_ffn[i], mods_p, w_router, b_router, tm_p, seq, n_tok, 0)
        h_all, los = moe_pre(xs, norm_ffn[i], mods_s, w_router, b_router, msz, msz, n_tok, mp, h_all)
        logits = jnp.concatenate([lop, los, jnp.zeros((n_route - n_tok, ROUTER_COLS), F32)], axis=0)
        e_ids, gate_w, rank, lane_counts = moe_route(logits, n_tok)
        slot, buf_tok, block_e, n_used, first, next_e = _dispatch(e_ids[:n_tok], rank[:n_tok], lane_counts)
        y = moe_experts(h_all, buf_tok, block_e, n_used, first, next_e, moe_w_gate, moe_w_up, moe_w_down, i)
        slot_flat = slot.reshape(-1)
        closing = norm_final if i == depth - 1 else None
        xp = moe_combine(xp, mods_p, y, slot_flat, gate_w, 0, te_p, seq, closing)
        xs = moe_combine(xs, mods_s, y, slot_flat, gate_w, mp, msz, msz, closing)

    y_prompt = xp.reshape(bsz, seq, d)
    y_sample = xs.reshape(db, ds, d)
    return (y_prompt, y_sample,
            jnp.stack(kp_l), jnp.stack(vp_l), jnp.stack(lfp_l), jnp.stack(gla_p),
            jnp.stack(ks_l), jnp.stack(vs_l), jnp.stack(lfs_l), jnp.stack(gla_s))
```
